```python
import jax, jax.numpy as jnp
from jax import lax
import numpy as np

D_MODEL = 1024
BATCH = 8
SEQ = 4096
DEPTH = 1

PLE_DIM = 256
RET_HEADS = 4
RET_HEAD_DIM = 128
RET_WIDTH = RET_HEADS * RET_HEAD_DIM
FOX_HEADS = 8
FOX_HEAD_DIM = 64
FOX_WIDTH = FOX_HEADS * FOX_HEAD_DIM
D_FF = 2816
RET_CHUNK = 128
Q_BLOCK = 128
ROPE_BASE = 10000.0
EPS = 1e-6
IN_SIZES = (RET_WIDTH,) * 4 + (FOX_WIDTH,) * 3 + (FOX_HEADS,)
IN_COLS = 4 * RET_WIDTH + 3 * FOX_WIDTH + FOX_HEADS

kernel_name = 'hybrid_retention_forgetting_macaron_block'


def rms_norm(x, g):
    x32 = x.astype(jnp.float32)
    y = x32 * lax.rsqrt(jnp.mean(x32 * x32, axis=-1, keepdims=True) + EPS)
    return (y * g.astype(jnp.float32)).astype(x.dtype)


def swiglu(x, w_gate, w_up, w_down):
    return (jax.nn.silu(x @ w_gate) * (x @ w_up)) @ w_down


def rotary(x, pos):
    half = x.shape[-1] // 2
    inv_freq = 1.0 / (ROPE_BASE ** (jnp.arange(half, dtype=jnp.float32) / half))
    ang = pos.astype(jnp.float32)[..., None] * inv_freq
    cos = jnp.cos(ang)[:, :, None, :]
    sin = jnp.sin(ang)[:, :, None, :]
    x1 = x[..., 0::2]
    x2 = x[..., 1::2]
    return jnp.stack([x1 * cos - x2 * sin, x1 * sin + x2 * cos], axis=-1).reshape(x.shape)


def chunkwise_retention(q, k, v):
    b, s, h, dk = q.shape
    dv = v.shape[-1]
    c = RET_CHUNK
    n = s // c
    to_chunks = lambda t: t.reshape(b, n, c, h, t.shape[-1]).transpose(0, 3, 1, 2, 4)
    q = to_chunks(q)
    k = to_chunks(k) * (dk ** -0.5)
    v = to_chunks(v)
    log_gamma = jnp.log1p(-jnp.exp2(-5.0 - jnp.arange(h, dtype=jnp.float32)))
    idx = jnp.arange(c, dtype=jnp.float32)
    diff = idx[:, None] - idx[None, :]
    dmask = jnp.where(diff >= 0, jnp.exp(log_gamma[:, None, None] * jnp.maximum(diff, 0.0)), 0.0)
    scores = jnp.einsum('bhncd,bhnmd->bhncm', q, k) * dmask[None, :, None]
    y_inner = jnp.einsum('bhncm,bhnme->bhnce', scores, v)
    k_dec = k * jnp.exp(log_gamma[:, None] * (c - 1 - idx))[None, :, None, :, None]
    kv = jnp.einsum('bhnmd,bhnme->nbhde', k_dec, v)
    chunk_decay = jnp.exp(log_gamma * c)[None, :, None, None]

    def step(state, kv_n):
        return chunk_decay * state + kv_n, state

    _, states = lax.scan(step, jnp.zeros((b, h, dk, dv), jnp.float32), kv)
    q_dec = q * jnp.exp(log_gamma[:, None] * (idx + 1.0))[None, :, None, :, None]
    y_cross = jnp.einsum('bhncd,nbhde->bhnce', q_dec, states)
    y = y_inner + y_cross
    return y.transpose(0, 2, 3, 1, 4).reshape(b, s, h, dv)


def head_group_norm(y):
    mu = jnp.mean(y, axis=-1, keepdims=True)
    var = jnp.mean(jnp.square(y - mu), axis=-1, keepdims=True)
    return (y - mu) * lax.rsqrt(var + EPS)


def forgetting_attention(q, k, v, log_f):
    b, s, h, d = q.shape
    nb = s // Q_BLOCK
    cum = jnp.cumsum(log_f, axis=1).transpose(0, 2, 1)
    qh = q.transpose(0, 2, 1, 3) * (d ** -0.5)
    kh = k.transpose(0, 2, 1, 3)
    vh = v.transpose(0, 2, 1, 3)
    q_blocks = qh.reshape(b, h, nb, Q_BLOCK, d).transpose(2, 0, 1, 3, 4)
    c_blocks = cum.reshape(b, h, nb, Q_BLOCK).transpose(2, 0, 1, 3)
    k_pos = jnp.arange(s)

    def block(args):
        q_n, c_n, n = args
        logits = jnp.einsum('bhqd,bhkd->bhqk', q_n, kh) + c_n[..., None] - cum[:, :, None, :]
        q_pos = n * Q_BLOCK + jnp.arange(Q_BLOCK)
        logits = jnp.where(k_pos[None, :] <= q_pos[:, None], logits, -jnp.inf)
        probs = jax.nn.softmax(logits, axis=-1)
        return jnp.einsum('bhqk,bhkd->bhqd', probs, vh)

    o = lax.map(block, (q_blocks, c_blocks, jnp.arange(nb)))
    return o.transpose(1, 0, 3, 2, 4).reshape(b, s, h, d)


def setup_inputs(seed: int = 0) -> dict:
    key = jax.random.key(seed)
    ks = jax.random.split(key, 24)
    f32 = jnp.float32

    def nrm(k, shape, fan_in):
        return jax.random.normal(k, shape, f32) * (fan_in ** -0.5)

    def gain(k, shape):
        return 1.0 + 0.05 * jax.random.normal(k, shape, f32)

    x = jax.random.normal(ks[0], (BATCH, SEQ, D_MODEL), f32)
    p = jax.random.normal(ks[1], (DEPTH, BATCH, SEQ, PLE_DIM), f32)
    positions = jnp.broadcast_to(jnp.arange(SEQ, dtype=jnp.int32)[None, :], (BATCH, SEQ))
    return {
        'x': x,
        'p': p,
        'positions': positions,
        'ln_ffn1': gain(ks[2], (DEPTH, D_MODEL)),
        'w_ffn1_gate': nrm(ks[3], (DEPTH, D_MODEL, D_FF), D_MODEL),
        'w_ffn1_up': nrm(ks[4], (DEPTH, D_MODEL, D_FF), D_MODEL),
        'w_ffn1_down': nrm(ks[5], (DEPTH, D_FF, D_MODEL), D_FF),
        'ln_mix': gain(ks[6], (DEPTH, D_MODEL)),
        'w_in': nrm(ks[7], (DEPTH, D_MODEL, IN_COLS), D_MODEL),
        'b_forget': jax.random.uniform(ks[8], (DEPTH, FOX_HEADS), f32, 1.0, 6.0),
        'w_merge': nrm(ks[9], (DEPTH, D_MODEL, 2 * D_MODEL), D_MODEL),
        'b_merge': 0.02 * jax.random.normal(ks[10], (DEPTH, 2 * D_MODEL), f32),
        'w_ret_out': nrm(ks[11], (DEPTH, RET_WIDTH, D_MODEL), RET_WIDTH),
        'w_fox_out': nrm(ks[12], (DEPTH, FOX_WIDTH, D_MODEL), FOX_WIDTH),
        'w_out': nrm(ks[13], (DEPTH, D_MODEL, D_MODEL), D_MODEL),
        'ln_ffn2': gain(ks[14], (DEPTH, D_MODEL)),
        'w_ffn2_gate': nrm(ks[15], (DEPTH, D_MODEL, D_FF), D_MODEL),
        'w_ffn2_up': nrm(ks[16], (DEPTH, D_MODEL, D_FF), D_MODEL),
        'w_ffn2_down': nrm(ks[17], (DEPTH, D_FF, D_MODEL), D_FF),
        'ln_ple': gain(ks[18], (DEPTH, D_MODEL)),
        'w_ple': nrm(ks[19], (DEPTH, PLE_DIM, D_MODEL), PLE_DIM),
        'w_ple_gate': nrm(ks[20], (DEPTH, D_MODEL, D_MODEL), D_MODEL),
        'ln_final': gain(ks[21], (D_MODEL,)),
    }


def reference(x, p, positions, ln_ffn1, w_ffn1_gate, w_ffn1_up, w_ffn1_down, ln_mix, w_in,
              b_forget, w_merge, b_merge, w_ret_out, w_fox_out, w_out, ln_ffn2, w_ffn2_gate,
              w_ffn2_up, w_ffn2_down, ln_ple, w_ple, w_ple_gate, ln_final):
    dt = x.dtype
    b, s, _ = x.shape
    split_at = np.cumsum(IN_SIZES)[:-1].tolist()
    h = x
    for i in range(DEPTH):
        h = h + 0.5 * swiglu(rms_norm(h, ln_ffn1[i]), w_ffn1_gate[i], w_ffn1_up[i], w_ffn1_down[i])

        u = rms_norm(h, ln_mix[i])
        proj = u @ w_in[i]
        r_q, r_k, r_v, r_g, f_q, f_k, f_v, f_f = jnp.split(proj, split_at, axis=-1)

        rq = rotary(r_q.astype(jnp.float32).reshape(b, s, RET_HEADS, RET_HEAD_DIM), positions)
        rk = rotary(r_k.astype(jnp.float32).reshape(b, s, RET_HEADS, RET_HEAD_DIM), positions)
        rv = r_v.astype(jnp.float32).reshape(b, s, RET_HEADS, RET_HEAD_DIM)
        y_ret = head_group_norm(chunkwise_retention(rq, rk, rv)).reshape(b, s, RET_WIDTH)
        y_ret = (y_ret * jax.nn.silu(r_g.astype(jnp.float32))).astype(dt)
        z_a = y_ret @ w_ret_out[i]

        log_f = jax.nn.log_sigmoid(f_f.astype(jnp.float32) + b_forget[i].astype(jnp.float32))
        fq = f_q.astype(jnp.float32).reshape(b, s, FOX_HEADS, FOX_HEAD_DIM)
        fk = f_k.astype(jnp.float32).reshape(b, s, FOX_HEADS, FOX_HEAD_DIM)
        fv = f_v.astype(jnp.float32).reshape(b, s, FOX_HEADS, FOX_HEAD_DIM)
        y_fox = forgetting_attention(fq, fk, fv, log_f).reshape(b, s, FOX_WIDTH).astype(dt)
        z_b = y_fox @ w_fox_out[i]

        gates = jax.nn.sigmoid(u @ w_merge[i] + b_merge[i])
        g_a, g_b = jnp.split(gates, 2, axis=-1)
        h = h + (g_a * z_a + g_b * z_b) @ w_out[i]

        h = h + 0.5 * swiglu(rms_norm(h, ln_ffn2[i]), w_ffn2_gate[i], w_ffn2_up[i], w_ffn2_down[i])

        ple_gate = jax.nn.sigmoid(rms_norm(h, ln_ple[i]) @ w_ple_gate[i])
        h = h + ple_gate * (p[i].astype(dt) @ w_ple[i])
    return rms_norm(h, ln_final)
```

```python
import functools

import numpy as np
import jax
import jax.numpy as jnp
from jax import lax
from jax.experimental import pallas as pl
from jax.experimental.pallas import tpu as pltpu

F32 = jnp.float32
BF16 = jnp.bfloat16

EPS = 1e-6
ROPE_BASE = 10000.0
RET_HEADS = 4
RET_HEAD_DIM = 128
RET_WIDTH = RET_HEADS * RET_HEAD_DIM
FOX_HEADS = 8
FOX_HEAD_DIM = 64
FOX_WIDTH = FOX_HEADS * FOX_HEAD_DIM
RET_CHUNK = 128

LANES = 128
FF_CHUNK = 256
NEG_BIG = -1e30

_NT = (((1,), (1,)), ((), ()))
_TN = (((0,), (0,)), ((), ()))


def _const_spec(shape):
    zeros = (0,) * len(shape)
    return pl.BlockSpec(shape, lambda *_: zeros, pipeline_mode=pl.Buffered(1))


def _params(n_axes, vmem_mb):
    return pltpu.CompilerParams(
        dimension_semantics=("arbitrary",) * n_axes,
        vmem_limit_bytes=vmem_mb * 1024 * 1024,
    )


def _rms_norm(x, g):
    return x * lax.rsqrt(jnp.mean(x * x, axis=-1, keepdims=True) + EPS) * g


def _swiglu_tile(xn, wgu_ref, wd_ref):
    acc = None
    for c in range(wgu_ref.shape[0]):
        gu = jnp.dot(xn, wgu_ref[c], preferred_element_type=F32)
        g = gu[:, :FF_CHUNK]
        u = gu[:, FF_CHUNK:]
        a = (g * jax.nn.sigmoid(g) * u).astype(BF16)
        d = jnp.dot(a, wd_ref[c], preferred_element_type=F32)
        acc = d if acc is None else acc + d
    return acc


def _ffn_kernel(x_ref, g_ref, wgu_ref, wd_ref, o_ref):
    x = x_ref[...]
    xn = _rms_norm(x, g_ref[...]).astype(BF16)
    o_ref[...] = x + 0.5 * _swiglu_tile(xn, wgu_ref, wd_ref)


def _ffn(x, g, wgu, wd, tm):
    t, d = x.shape
    return pl.pallas_call(
        _ffn_kernel,
        grid=(t // tm,),
        in_specs=[
            pl.BlockSpec((tm, d), lambda i: (i, 0)),
            _const_spec(g.shape),
            _const_spec(wgu.shape),
            _const_spec(wd.shape),
        ],
        out_specs=pl.BlockSpec((tm, d), lambda i: (i, 0)),
        out_shape=jax.ShapeDtypeStruct((t, d), F32),
        compiler_params=_params(1, 48),
        name="ffn",
    )(x, g, wgu, wd)


def _mix_in_kernel(h_ref, g_ref, pos_ref, invf_ref, sgn_ref, wr_ref, wf_ref, wfft_ref, bf_ref,
                   wm_ref, bm_ref, r_ref, f_ref, c_ref, gate_ref, carry_ref, *, ret_scale, fox_scale):
    tm = h_ref.shape[1]
    u = _rms_norm(h_ref[0], g_ref[...]).astype(BF16)

    pr = jnp.dot(u, wr_ref[...], preferred_element_type=F32)
    ang = pos_ref[0].astype(F32) * invf_ref[...]
    cos = jnp.cos(ang)
    sin = jnp.sin(ang) * sgn_ref[...]
    for part, scale in ((0, ret_scale), (1, 1.0)):
        for hd in range(RET_HEADS):
            lo = part * RET_WIDTH + hd * RET_HEAD_DIM
            xh = pr[:, lo:lo + RET_HEAD_DIM]
            rot = xh * cos + pltpu.roll(xh, RET_HEAD_DIM // 2, axis=1) * sin
            if scale != 1.0:
                rot = rot * scale
            r_ref[0, :, lo:lo + RET_HEAD_DIM] = rot.astype(BF16)
    r_ref[0, :, 2 * RET_WIDTH:3 * RET_WIDTH] = pr[:, 2 * RET_WIDTH:3 * RET_WIDTH].astype(BF16)
    rg = pr[:, 3 * RET_WIDTH:]
    r_ref[0, :, 3 * RET_WIDTH:] = (rg * jax.nn.sigmoid(rg)).astype(BF16)

    pf = jnp.dot(u, wf_ref[...], preferred_element_type=F32)
    f_ref[0, :, :FOX_WIDTH] = (pf[:, :FOX_WIDTH] * fox_scale).astype(BF16)
    f_ref[0, :, FOX_WIDTH:] = pf[:, FOX_WIDTH:].astype(BF16)

    ff = lax.dot_general(wfft_ref[...], u, _NT, preferred_element_type=F32) + bf_ref[...]
    logf = jnp.minimum(ff, 0.0) - jnp.log1p(jnp.exp(-jnp.abs(ff)))
    lane = lax.broadcasted_iota(jnp.int32, logf.shape, 1)
    cum = logf
    sh = 1
    while sh < tm:
        cum = cum + jnp.where(lane >= sh, pltpu.roll(cum, sh, axis=1), 0.0)
        sh *= 2

    @pl.when(pl.program_id(1) == 0)
    def _():
        carry_ref[...] = jnp.zeros_like(carry_ref)

    cum = cum + carry_ref[:, 0:1]
    c_ref[0] = cum
    carry_ref[...] = jnp.broadcast_to(cum[:, tm - 1:tm], carry_ref.shape)

    gm = jnp.dot(u, wm_ref[...], preferred_element_type=F32) + bm_ref[...]
    gate_ref[0] = jax.nn.sigmoid(gm).astype(BF16)


def _mix_in(h, g, pos, invf, sgn, wr, wf, wfft, bfg, wm, bm, tm):
    b, s, d = h.shape
    kern = functools.partial(_mix_in_kernel, ret_scale=RET_HEAD_DIM ** -0.5, fox_scale=FOX_HEAD_DIM ** -0.5)
    tile = lambda w: pl.BlockSpec((1, tm, w), lambda i, j: (i, j, 0))
    return pl.pallas_call(
        kern,
        grid=(b, s // tm),
        in_specs=[tile(d), _const_spec(g.shape), tile(1), _const_spec(invf.shape), _const_spec(sgn.shape),
                  _const_spec(wr.shape), _const_spec(wf.shape), _const_spec(wfft.shape), _const_spec(bfg.shape),
                  _const_spec(wm.shape), _const_spec(bm.shape)],
        out_specs=[tile(4 * RET_WIDTH), tile(3 * FOX_WIDTH),
                   pl.BlockSpec((1, FOX_HEADS, tm), lambda i, j: (i, 0, j)), tile(2 * d)],
        out_shape=[jax.ShapeDtypeStruct((b, s, 4 * RET_WIDTH), BF16),
                   jax.ShapeDtypeStruct((b, s, 3 * FOX_WIDTH), BF16),
                   jax.ShapeDtypeStruct((b, FOX_HEADS, s), F32),
                   jax.ShapeDtypeStruct((b, s, 2 * d), BF16)],
        scratch_shapes=[pltpu.VMEM((FOX_HEADS, LANES), F32)],
        compiler_params=_params(2, 48),
        name="mix_in",
    )(h, g, pos, invf, sgn, wr, wf, wfft, bfg, wm, bm)


def _retention_kernel(q_ref, k_ref, v_ref, g_ref, dmask_ref, qdec_ref, kdec_ref, o_ref, state_ref, *, chunk_decay):
    tr = q_ref.shape[1]

    @pl.when(pl.program_id(1) == 0)
    def _():
        state_ref[...] = jnp.zeros_like(state_ref)

    for c in range(tr // RET_CHUNK):
        rows = slice(c * RET_CHUNK, (c + 1) * RET_CHUNK)
        for hd in range(RET_HEADS):
            cols = slice(hd * RET_HEAD_DIM, (hd + 1) * RET_HEAD_DIM)
            q = q_ref[0, rows, cols]
            k = k_ref[0, rows, cols]
            v = v_ref[0, rows, cols]
            state = state_ref[hd]
            scores = lax.dot_general(q, k, _NT, preferred_element_type=F32) * dmask_ref[hd]
            q_dec = (q.astype(F32) * qdec_ref[hd]).astype(BF16)
            y = (jnp.dot(scores.astype(BF16), v, preferred_element_type=F32)
                 + jnp.dot(q_dec, state.astype(BF16), preferred_element_type=F32))
            k_dec = (k.astype(F32) * kdec_ref[hd]).astype(BF16)
            state_ref[hd] = chunk_decay[hd] * state + lax.dot_general(k_dec, v, _TN, preferred_element_type=F32)
            mu = jnp.mean(y, axis=-1, keepdims=True)
            yc = y - mu
            var = jnp.mean(yc * yc, axis=-1, keepdims=True)
            yn = yc * lax.rsqrt(var + EPS)
            o_ref[0, rows, cols] = (yn * g_ref[0, rows, cols].astype(F32)).astype(BF16)


def _retention(r, dmask, qdec, kdec, chunk_decay, tr):
    b, s, _ = r.shape
    kern = functools.partial(_retention_kernel, chunk_decay=chunk_decay)
    part = lambda p: pl.BlockSpec((1, tr, RET_WIDTH), lambda i, j: (i, j, p))
    return pl.pallas_call(
        kern,
        grid=(b, s // tr),
        in_specs=[part(0), part(1), part(2), part(3),
                  _const_spec(dmask.shape), _const_spec(qdec.shape), _const_spec(kdec.shape)],
        out_specs=pl.BlockSpec((1, tr, RET_WIDTH), lambda i, j: (i, j, 0)),
        out_shape=jax.ShapeDtypeStruct((b, s, RET_WIDTH), BF16),
        scratch_shapes=[pltpu.VMEM((RET_HEADS, RET_HEAD_DIM, RET_HEAD_DIM), F32)],
        compiler_params=_params(2, 32),
        name="retention",
    )(r, r, r, r, dmask, qdec, kdec)


def _fox_kernel(q_ref, k_ref, v_ref, c_ref, o_ref, *, tq):
    s_len = q_ref.shape[1]
    nq = s_len // tq
    lane = lax.broadcasted_iota(jnp.int32, (tq, LANES), 1)
    first = lane < FOX_HEAD_DIM
    row = lax.broadcasted_iota(jnp.int32, (tq, tq), 0)
    col = lax.broadcasted_iota(jnp.int32, (tq, tq), 1)
    causal = col <= row

    def q_block(qi, _):
        q0 = pl.multiple_of(qi * tq, tq)
        q2 = q_ref[0, pl.ds(q0, tq), :]
        zero = jnp.zeros_like(q2)
        q_heads = (jnp.where(first, q2, zero), jnp.where(first, zero, q2))

        def step(ki, carry, masked):
            k0 = pl.multiple_of(ki * tq, tq)
            k2 = k_ref[0, pl.ds(k0, tq), :]
            v2 = v_ref[0, pl.ds(k0, tq), :]
            new = []
            pvs = []
            alphas = []
            for hd in range(2):
                m, l = carry[2 * hd], carry[2 * hd + 1]
                cs = c_ref[0, 0, hd:hd + 1, pl.ds(k0, tq)]
                s = lax.dot_general(q_heads[hd], k2, _NT, preferred_element_type=F32) - cs
                if masked:
                    s = jnp.where(causal, s, NEG_BIG)
                m_new = jnp.maximum(m, jnp.max(s, axis=-1, keepdims=True))
                alpha = jnp.exp(m - m_new)
                p = jnp.exp(s - m_new)
                l_new = alpha * l + jnp.sum(p, axis=-1, keepdims=True)
                pvs.append(jnp.dot(p.astype(BF16), v2, preferred_element_type=F32))
                alphas.append(alpha)
                new += [m_new, l_new]
            acc = carry[4]
            acc = jnp.where(first, alphas[0] * acc + pvs[0], alphas[1] * acc + pvs[1])
            return (*new, acc)

        col1 = lambda val: jnp.full((tq, 1), val, F32)
        init = (col1(NEG_BIG), col1(0.0), col1(NEG_BIG), col1(0.0), jnp.zeros((tq, LANES), F32))
        carry = lax.fori_loop(0, qi, lambda ki, cr: step(ki, cr, False), init)
        _, l_a, _, l_b, acc = step(qi, carry, True)
        out = acc / jnp.where(first, l_a, l_b)
        o_ref[0, pl.ds(q0, tq), :] = out.astype(BF16)
        return 0

    lax.fori_loop(0, nq, q_block, 0)


def _fox(f, c, tq):
    b, s, _ = f.shape
    pairs = FOX_WIDTH // LANES
    kern = functools.partial(_fox_kernel, tq=tq)
    part = lambda p: pl.BlockSpec((1, s, LANES), lambda i, j: (i, 0, p * pairs + j))
    return pl.pallas_call(
        kern,
        grid=(b, pairs),
        in_specs=[part(0), part(1), part(2),
                  pl.BlockSpec((1, 1, 2, s), lambda i, j: (i, j, 0, 0))],
        out_specs=pl.BlockSpec((1, s, LANES), lambda i, j: (i, 0, j)),
        out_shape=jax.ShapeDtypeStruct((b, s, FOX_WIDTH), BF16),
        compiler_params=_params(2, 32),
        name="fox",
    )(f, f, f, c)


def _merge_kernel(h_ref, yr_ref, yf_ref, gate_ref, wro_ref, wfo_ref, wo_ref, o_ref):
    d = h_ref.shape[1]
    z_a = jnp.dot(yr_ref[...], wro_ref[...], preferred_element_type=F32)
    z_b = jnp.dot(yf_ref[...], wfo_ref[...], preferred_element_type=F32)
    mixed = gate_ref[:, :d].astype(F32) * z_a + gate_ref[:, d:].astype(F32) * z_b
    o_ref[...] = h_ref[...] + jnp.dot(mixed.astype(BF16), wo_ref[...], preferred_element_type=F32)


def _merge(h, yr, yf, gates, wro, wfo, wo, tm):
    t, d = h.shape
    tile = lambda w: pl.BlockSpec((tm, w), lambda i: (i, 0))
    return pl.pallas_call(
        _merge_kernel,
        grid=(t // tm,),
        in_specs=[tile(d), tile(RET_WIDTH), tile(FOX_WIDTH), tile(2 * d),
                  _const_spec(wro.shape), _const_spec(wfo.shape), _const_spec(wo.shape)],
        out_specs=tile(d),
        out_shape=jax.ShapeDtypeStruct((t, d), F32),
        compiler_params=_params(1, 40),
        name="merge",
    )(h, yr, yf, gates, wro, wfo, wo)


def _ple_final_kernel(h_ref, p_ref, gp_ref, wpg_ref, wp_ref, gf_ref, o_ref):
    h = h_ref[...]
    hn = _rms_norm(h, gp_ref[...]).astype(BF16)
    gate = jax.nn.sigmoid(jnp.dot(hn, wpg_ref[...], preferred_element_type=F32))
    emb = jnp.dot(p_ref[...].astype(BF16), wp_ref[...], preferred_element_type=F32)
    o_ref[...] = _rms_norm(h + gate * emb, gf_ref[...])


def _ple_final(h, p, gp, wpg, wp, gf, tm):
    t, d = h.shape
    tile = lambda w: pl.BlockSpec((tm, w), lambda i: (i, 0))
    return pl.pallas_call(
        _ple_final_kernel,
        grid=(t // tm,),
        in_specs=[tile(d), tile(p.shape[1]), _const_spec(gp.shape), _const_spec(wpg.shape),
                  _const_spec(wp.shape), _const_spec(gf.shape)],
        out_specs=tile(d),
        out_shape=jax.ShapeDtypeStruct((t, d), F32),
        compiler_params=_params(1, 40),
        name="ple_final",
    )(h, p, gp, wpg, wp, gf)


def _ffn_weights(w_gate, w_up, w_down):
    d, d_ff = w_gate.shape
    nc = d_ff // FF_CHUNK
    wgu = jnp.concatenate([w_gate.reshape(d, nc, FF_CHUNK), w_up.reshape(d, nc, FF_CHUNK)], axis=-1)
    return wgu.transpose(1, 0, 2).astype(BF16), w_down.reshape(nc, FF_CHUNK, d).astype(BF16)


def _retention_tables():
    hd = np.arange(RET_HEADS, dtype=np.float64)
    log_gamma = np.log1p(-np.exp2(-5.0 - hd))
    idx = np.arange(RET_CHUNK, dtype=np.float64)
    diff = idx[:, None] - idx[None, :]
    dmask = np.where(diff >= 0, np.exp(log_gamma[:, None, None] * np.maximum(diff, 0.0)), 0.0)
    ones = np.ones((1, 1, RET_HEAD_DIM))
    qdec = np.exp(log_gamma[:, None] * (idx + 1.0))[:, :, None] * ones
    kdec = np.exp(log_gamma[:, None] * (RET_CHUNK - 1 - idx))[:, :, None] * ones
    chunk_decay = tuple(float(x) for x in np.exp(log_gamma * RET_CHUNK))
    as_f32 = lambda a: jnp.asarray(a, dtype=F32)
    return as_f32(dmask), as_f32(qdec), as_f32(kdec), chunk_decay


def _layer(h, p, positions, ln_ffn1, w_ffn1_gate, w_ffn1_up, w_ffn1_down, ln_mix, w_in, b_forget, w_merge,
           b_merge, w_ret_out, w_fox_out, w_out, ln_ffn2, w_ffn2_gate, w_ffn2_up, w_ffn2_down, ln_ple, w_ple,
           w_ple_gate, ln_final):
    b, s, d = h.shape
    t = b * s
    tm = min(512, s)
    row = lambda v: v.reshape(1, -1).astype(F32)

    wgu1, wd1 = _ffn_weights(w_ffn1_gate, w_ffn1_up, w_ffn1_down)
    h1 = _ffn(h.reshape(t, d), row(ln_ffn1), wgu1, wd1, tm)

    even_odd = lambda w: w.reshape(d, RET_HEADS, RET_HEAD_DIM // 2, 2).transpose(0, 1, 3, 2).reshape(d, RET_WIDTH)
    w_rq = even_odd(w_in[:, :RET_WIDTH])
    w_rk = even_odd(w_in[:, RET_WIDTH:2 * RET_WIDTH])
    fox_lo = 4 * RET_WIDTH
    fox_hi = fox_lo + 3 * FOX_WIDTH
    wr = jnp.concatenate([w_rq, w_rk, w_in[:, 2 * RET_WIDTH:fox_lo]], axis=1).astype(BF16)
    wf = w_in[:, fox_lo:fox_hi].astype(BF16)
    wfft = w_in[:, fox_hi:].T.astype(BF16)
    inv_freq = 1.0 / (ROPE_BASE ** (jnp.arange(RET_HEAD_DIM // 2, dtype=F32) / (RET_HEAD_DIM // 2)))
    invf = jnp.concatenate([inv_freq, inv_freq]).reshape(1, RET_HEAD_DIM)
    sgn = jnp.concatenate([-jnp.ones((RET_HEAD_DIM // 2,), F32), jnp.ones((RET_HEAD_DIM // 2,), F32)]).reshape(1, -1)
    r, f, c, gates = _mix_in(h1.reshape(b, s, d), row(ln_mix), positions.reshape(b, s, 1), invf, sgn, wr, wf, wfft,
                             b_forget.reshape(FOX_HEADS, 1).astype(F32), w_merge.astype(BF16), row(b_merge), tm)

    dmask, qdec, kdec, chunk_decay = _retention_tables()
    y_ret = _retention(r, dmask, qdec, kdec, chunk_decay, min(512, s))
    y_fox = _fox(f, c.reshape(b, FOX_WIDTH // LANES, 2, s), min(256, s))

    h2 = _merge(h1, y_ret.reshape(t, RET_WIDTH), y_fox.reshape(t, FOX_WIDTH), gates.reshape(t, 2 * d),
                w_ret_out.astype(BF16), w_fox_out.astype(BF16), w_out.astype(BF16), tm)

    wgu2, wd2 = _ffn_weights(w_ffn2_gate, w_ffn2_up, w_ffn2_down)
    h3 = _ffn(h2, row(ln_ffn2), wgu2, wd2, tm)

    gf = row(ln_final) if ln_final is not None else None
    assert gf is not None, "multi-layer stacks need a ple kernel without the final norm"
    out = _ple_final(h3, p.reshape(t, -1), row(ln_ple), w_ple_gate.astype(BF16), w_ple.astype(BF16), gf, tm)
    return out.reshape(b, s, d)


def kernel(x, p, positions, ln_ffn1, w_ffn1_gate, w_ffn1_up, w_ffn1_down, ln_mix, w_in, b_forget, w_merge, b_merge,
           w_ret_out, w_fox_out, w_out, ln_ffn2, w_ffn2_gate, w_ffn2_up, w_ffn2_down, ln_ple, w_ple, w_ple_gate,
           ln_final):
    depth = p.shape[0]
    assert depth == 1, "this implementation covers the single-layer configuration"
    return _layer(x, p[0], positions, ln_ffn1[0], w_ffn1_gate[0], w_ffn1_up[0], w_ffn1_down[0], ln_mix[0], w_in[0],
                  b_forget[0], w_merge[0], b_merge[0], w_ret_out[0], w_fox_out[0], w_out[0], ln_ffn2[0],
                  w_ffn2_gate[0], w_ffn2_up[0], w_ffn2_down[0], ln_ple[0], w_ple[0], w_ple_gate[0], ln_final)
```

```python
import functools

import numpy as np
import jax
import jax.numpy as jnp
from jax import lax
from jax.experimental import pallas as pl
from jax.experimental.pallas import tpu as pltpu

F32 = jnp.float32
BF16 = jnp.bfloat16

EPS = 1e-6
ROPE_BASE = 10000.0
RET_HEADS = 4
RET_HEAD_DIM = 128
RET_WIDTH = RET_HEADS * RET_HEAD_DIM
FOX_HEADS = 8
FOX_HEAD_DIM = 64
FOX_WIDTH = FOX_HEADS * FOX_HEAD_DIM
RET_CHUNK = 128

LANES = 128
FF_CHUNK = 256
NEG_BIG = -1e30
LOG2E = 1.4426950408889634

_NT = (((1,), (1,)), ((), ()))
_TN = (((0,), (0,)), ((), ()))


def _const_spec(shape):
    zeros = (0,) * len(shape)
    return pl.BlockSpec(shape, lambda *_: zeros, pipeline_mode=pl.Buffered(1))


def _params(n_axes, vmem_mb):
    return pltpu.CompilerParams(
        dimension_semantics=("arbitrary",) * n_axes,
        vmem_limit_bytes=vmem_mb * 1024 * 1024,
    )


def _rms_norm(x, g):
    return x * lax.rsqrt(jnp.mean(x * x, axis=-1, keepdims=True) + EPS) * g


def _swiglu_tile(xn, wgu_ref, wd_ref):
    acc = None
    for c in range(wgu_ref.shape[0]):
        gu = jnp.dot(xn, wgu_ref[c], preferred_element_type=F32)
        g = gu[:, :FF_CHUNK]
        u = gu[:, FF_CHUNK:]
        a = (g * jax.nn.sigmoid(g) * u).astype(BF16)
        d = jnp.dot(a, wd_ref[c], preferred_element_type=F32)
        acc = d if acc is None else acc + d
    return acc


def _ffn_kernel(x_ref, g_ref, wgu_ref, wd_ref, o_ref):
    x = x_ref[...]
    xn = _rms_norm(x, g_ref[...]).astype(BF16)
    o_ref[...] = x + 0.5 * _swiglu_tile(xn, wgu_ref, wd_ref)


def _ffn(x, g, wgu, wd, tm):
    t, d = x.shape
    return pl.pallas_call(
        _ffn_kernel,
        grid=(t // tm,),
        in_specs=[
            pl.BlockSpec((tm, d), lambda i: (i, 0)),
            _const_spec(g.shape),
            _const_spec(wgu.shape),
            _const_spec(wd.shape),
        ],
        out_specs=pl.BlockSpec((tm, d), lambda i: (i, 0)),
        out_shape=jax.ShapeDtypeStruct((t, d), F32),
        compiler_params=_params(1, 48),
        name="ffn",
    )(x, g, wgu, wd)


def _mix_in_kernel(h_ref, g_ref, pos_ref, invf_ref, sgn_ref, wr_ref, wfk_ref, wqvt_ref, wff_ref, bf_ref,
                   wm_ref, bm_ref, r_ref, fk_ref, fqvt_ref, c_ref, gate_ref, carry_ref, *, ret_scale, fox_scale):
    tm = h_ref.shape[1]
    u = _rms_norm(h_ref[0], g_ref[...]).astype(BF16)

    pr = jnp.dot(u, wr_ref[...], preferred_element_type=F32)
    ang = pos_ref[0].astype(F32) * invf_ref[...]
    cos = jnp.cos(ang)
    sin = jnp.sin(ang) * sgn_ref[...]
    for part, scale in ((0, ret_scale), (1, 1.0)):
        for hd in range(RET_HEADS):
            lo = part * RET_WIDTH + hd * RET_HEAD_DIM
            xh = pr[:, lo:lo + RET_HEAD_DIM]
            rot = xh * cos + pltpu.roll(xh, RET_HEAD_DIM // 2, axis=1) * sin
            if scale != 1.0:
                rot = rot * scale
            r_ref[0, :, lo:lo + RET_HEAD_DIM] = rot.astype(BF16)
    r_ref[0, :, 2 * RET_WIDTH:3 * RET_WIDTH] = pr[:, 2 * RET_WIDTH:3 * RET_WIDTH].astype(BF16)
    rg = pr[:, 3 * RET_WIDTH:]
    r_ref[0, :, 3 * RET_WIDTH:] = (rg * jax.nn.sigmoid(rg)).astype(BF16)

    fk_ref[0] = jnp.dot(u, wfk_ref[...], preferred_element_type=F32).astype(BF16)
    qvt = lax.dot_general(wqvt_ref[...], u, _NT, preferred_element_type=F32)
    fqvt_ref[0, :FOX_WIDTH, :] = (qvt[:FOX_WIDTH] * fox_scale).astype(BF16)
    fqvt_ref[0, FOX_WIDTH:, :] = qvt[FOX_WIDTH:].astype(BF16)

    ff = jnp.dot(u, wff_ref[...], preferred_element_type=F32) + bf_ref[...]
    logf = jnp.minimum(ff, 0.0) - jnp.log1p(jnp.exp(-jnp.abs(ff)))
    tok = lax.broadcasted_iota(jnp.int32, logf.shape, 0)
    cum = logf
    sh = 1
    while sh < tm:
        cum = cum + jnp.where(tok >= sh, pltpu.roll(cum, sh, axis=0), 0.0)
        sh *= 2

    @pl.when(pl.program_id(1) == 0)
    def _():
        carry_ref[...] = jnp.zeros_like(carry_ref)

    cum = cum + carry_ref[0:1, :]
    c_ref[0] = cum
    carry_ref[...] = jnp.broadcast_to(cum[tm - 1:tm, :], carry_ref.shape)

    gm = jnp.dot(u, wm_ref[...], preferred_element_type=F32) + bm_ref[...]
    gate_ref[0] = jax.nn.sigmoid(gm).astype(BF16)


def _mix_in(h, g, pos, invf, sgn, wr, wfk, wqvt, wff, bfg, wm, bm, tm):
    b, s, d = h.shape
    kern = functools.partial(_mix_in_kernel, ret_scale=RET_HEAD_DIM ** -0.5,
                             fox_scale=FOX_HEAD_DIM ** -0.5 * LOG2E)
    tile = lambda w: pl.BlockSpec((1, tm, w), lambda i, j: (i, j, 0))
    return pl.pallas_call(
        kern,
        grid=(b, s // tm),
        in_specs=[tile(d), _const_spec(g.shape), tile(1), _const_spec(invf.shape), _const_spec(sgn.shape),
                  _const_spec(wr.shape), _const_spec(wfk.shape), _const_spec(wqvt.shape), _const_spec(wff.shape),
                  _const_spec(bfg.shape), _const_spec(wm.shape), _const_spec(bm.shape)],
        out_specs=[tile(4 * RET_WIDTH), tile(FOX_WIDTH),
                   pl.BlockSpec((1, 2 * FOX_WIDTH, tm), lambda i, j: (i, 0, j)), tile(LANES), tile(2 * d)],
        out_shape=[jax.ShapeDtypeStruct((b, s, 4 * RET_WIDTH), BF16),
                   jax.ShapeDtypeStruct((b, s, FOX_WIDTH), BF16),
                   jax.ShapeDtypeStruct((b, 2 * FOX_WIDTH, s), BF16),
                   jax.ShapeDtypeStruct((b, s, LANES), F32),
                   jax.ShapeDtypeStruct((b, s, 2 * d), BF16)],
        scratch_shapes=[pltpu.VMEM((8, LANES), F32)],
        compiler_params=_params(2, 48),
        name="mix_in",
    )(h, g, pos, invf, sgn, wr, wfk, wqvt, wff, bfg, wm, bm)


def _retention_kernel(q_ref, k_ref, v_ref, g_ref, dmask_ref, qdec_ref, kdec_ref, o_ref, state_ref, *, chunk_decay):
    tr = q_ref.shape[1]

    @pl.when(pl.program_id(1) == 0)
    def _():
        state_ref[...] = jnp.zeros_like(state_ref)

    for c in range(tr // RET_CHUNK):
        rows = slice(c * RET_CHUNK, (c + 1) * RET_CHUNK)
        for hd in range(RET_HEADS):
            cols = slice(hd * RET_HEAD_DIM, (hd + 1) * RET_HEAD_DIM)
            q = q_ref[0, rows, cols]
            k = k_ref[0, rows, cols]
            v = v_ref[0, rows, cols]
            state = state_ref[hd]
            scores = lax.dot_general(q, k, _NT, preferred_element_type=F32) * dmask_ref[hd]
            q_dec = (q.astype(F32) * qdec_ref[hd]).astype(BF16)
            y = (jnp.dot(scores.astype(BF16), v, preferred_element_type=F32)
                 + jnp.dot(q_dec, state.astype(BF16), preferred_element_type=F32))
            k_dec = (k.astype(F32) * kdec_ref[hd]).astype(BF16)
            state_ref[hd] = chunk_decay[hd] * state + lax.dot_general(k_dec, v, _TN, preferred_element_type=F32)
            mu = jnp.mean(y, axis=-1, keepdims=True)
            yc = y - mu
            var = jnp.mean(yc * yc, axis=-1, keepdims=True)
            yn = yc * lax.rsqrt(var + EPS)
            o_ref[0, rows, cols] = (yn * g_ref[0, rows, cols].astype(F32)).astype(BF16)


def _retention(r, dmask, qdec, kdec, chunk_decay, tr):
    b, s, _ = r.shape
    kern = functools.partial(_retention_kernel, chunk_decay=chunk_decay)
    part = lambda p: pl.BlockSpec((1, tr, RET_WIDTH), lambda i, j: (i, j, p))
    return pl.pallas_call(
        kern,
        grid=(b, s // tr),
        in_specs=[part(0), part(1), part(2), part(3),
                  _const_spec(dmask.shape), _const_spec(qdec.shape), _const_spec(kdec.shape)],
        out_specs=pl.BlockSpec((1, tr, RET_WIDTH), lambda i, j: (i, j, 0)),
        out_shape=jax.ShapeDtypeStruct((b, s, RET_WIDTH), BF16),
        scratch_shapes=[pltpu.VMEM((RET_HEADS, RET_HEAD_DIM, RET_HEAD_DIM), F32)],
        compiler_params=_params(2, 32),
        name="retention",
    )(r, r, r, r, dmask, qdec, kdec)


def _fox_kernel(qt_ref, k_ref, vt_ref, c_ref, o_ref, cb_ref, *, tq):
    s_len = k_ref.shape[1]
    nq = s_len // tq
    pair = pl.program_id(1)

    lane = lax.broadcasted_iota(jnp.int32, (tq, LANES), 1)
    for hd in range(2):
        def fill(i, _, hd=hd):
            r0 = pl.multiple_of(i * tq, tq)
            blk = jnp.where(lane == 2 * pair + hd, c_ref[0, pl.ds(r0, tq), :], 0.0)
            col = jnp.sum(blk, axis=1, keepdims=True) * LOG2E
            cb_ref[hd, pl.ds(r0, tq), :] = jnp.broadcast_to(col, (tq, LANES))
            return 0
        lax.fori_loop(0, nq, fill, 0)

    key_idx = lax.broadcasted_iota(jnp.int32, (tq, tq), 0)
    qry_idx = lax.broadcasted_iota(jnp.int32, (tq, tq), 1)
    causal = key_idx <= qry_idx
    half_zero = jnp.zeros((FOX_HEAD_DIM, tq), BF16)

    def q_block(qi, _):
        q0 = pl.multiple_of(qi * tq, tq)
        qt = qt_ref[0, :, pl.ds(q0, tq)]
        qts = (jnp.concatenate([qt[:FOX_HEAD_DIM], half_zero], axis=0),
               jnp.concatenate([half_zero, qt[FOX_HEAD_DIM:]], axis=0))

        def scores(ki):
            k0 = pl.multiple_of(ki * tq, tq)
            k2 = k_ref[0, pl.ds(k0, tq), :]
            return tuple(jnp.dot(k2, qts[hd], preferred_element_type=F32) for hd in range(2))

        def consume(ki, qk, stats, masked):
            k0 = pl.multiple_of(ki * tq, tq)
            probs, out = [], []
            for hd in range(2):
                m, l = stats[3 * hd:3 * hd + 2]
                cb = cb_ref[hd, pl.ds(k0, tq), :]
                s = qk[hd] - jnp.concatenate([cb] * (tq // LANES), axis=1)
                if masked:
                    s = jnp.where(causal, s, NEG_BIG)
                m_new = jnp.maximum(m, jnp.max(s, axis=0, keepdims=True))
                alpha = jnp.exp2(m - m_new)
                p = jnp.exp2(s - m_new)
                probs.append((alpha, p.astype(BF16)))
                out += [m_new, alpha * l + jnp.sum(p, axis=0, keepdims=True), None]
            for hd in range(2):
                alpha, p = probs[hd]
                vt = vt_ref[0, hd * FOX_HEAD_DIM:(hd + 1) * FOX_HEAD_DIM, pl.ds(k0, tq)]
                out[3 * hd + 2] = alpha * stats[3 * hd + 2] + jnp.dot(vt, p, preferred_element_type=F32)
            return tuple(out)

        def step(ki, carry):
            nxt = scores(ki + 1)
            return nxt + consume(ki, carry[:2], carry[2:], False)

        row = lambda val: jnp.full((1, tq), val, F32)
        init = scores(0) + (row(NEG_BIG), row(0.0), jnp.zeros((FOX_HEAD_DIM, tq), F32)) * 2
        carry = lax.fori_loop(0, qi, step, init)
        _, l_a, acc_a, _, l_b, acc_b = consume(qi, carry[:2], carry[2:], True)
        out_t = jnp.concatenate([acc_a / l_a, acc_b / l_b], axis=0)
        o_ref[0, pl.ds(q0, tq), :] = out_t.T.astype(BF16)
        return 0

    lax.fori_loop(0, nq, q_block, 0)


def _fox(fk, fqvt, c, tq):
    b, s, _ = fk.shape
    pairs = FOX_WIDTH // LANES
    kern = functools.partial(_fox_kernel, tq=tq)
    return pl.pallas_call(
        kern,
        grid=(b, pairs),
        in_specs=[pl.BlockSpec((1, LANES, s), lambda i, j: (i, j, 0)),
                  pl.BlockSpec((1, s, LANES), lambda i, j: (i, 0, j)),
                  pl.BlockSpec((1, LANES, s), lambda i, j: (i, pairs + j, 0)),
                  pl.BlockSpec((1, s, LANES), lambda i, j: (i, 0, 0))],
        out_specs=pl.BlockSpec((1, s, LANES), lambda i, j: (i, 0, j)),
        out_shape=jax.ShapeDtypeStruct((b, s, FOX_WIDTH), BF16),
        scratch_shapes=[pltpu.VMEM((2, s, LANES), F32)],
        compiler_params=_params(2, 32),
        name="fox",
    )(fqvt, fk, fqvt, c)


def _merge_kernel(h_ref, yr_ref, yf_ref, gate_ref, wro_ref, wfo_ref, wo_ref, o_ref):
    d = h_ref.shape[1]
    z_a = jnp.dot(yr_ref[...], wro_ref[...], preferred_element_type=F32)
    z_b = jnp.dot(yf_ref[...], wfo_ref[...], preferred_element_type=F32)
    mixed = gate_ref[:, :d].astype(F32) * z_a + gate_ref[:, d:].astype(F32) * z_b
    o_ref[...] = h_ref[...] + jnp.dot(mixed.astype(BF16), wo_ref[...], preferred_element_type=F32)


def _merge(h, yr, yf, gates, wro, wfo, wo, tm):
    t, d = h.shape
    tile = lambda w: pl.BlockSpec((tm, w), lambda i: (i, 0))
    return pl.pallas_call(
        _merge_kernel,
        grid=(t // tm,),
        in_specs=[tile(d), tile(RET_WIDTH), tile(FOX_WIDTH), tile(2 * d),
                  _const_spec(wro.shape), _const_spec(wfo.shape), _const_spec(wo.shape)],
        out_specs=tile(d),
        out_shape=jax.ShapeDtypeStruct((t, d), F32),
        compiler_params=_params(1, 40),
        name="merge",
    )(h, yr, yf, gates, wro, wfo, wo)


def _ple_final_kernel(h_ref, p_ref, gp_ref, wpg_ref, wp_ref, gf_ref, o_ref):
    h = h_ref[...]
    hn = _rms_norm(h, gp_ref[...]).astype(BF16)
    gate = jax.nn.sigmoid(jnp.dot(hn, wpg_ref[...], preferred_element_type=F32))
    emb = jnp.dot(p_ref[...].astype(BF16), wp_ref[...], preferred_element_type=F32)
    o_ref[...] = _rms_norm(h + gate * emb, gf_ref[...])


def _ple_final(h, p, gp, wpg, wp, gf, tm):
    t, d = h.shape
    tile = lambda w: pl.BlockSpec((tm, w), lambda i: (i, 0))
    return pl.pallas_call(
        _ple_final_kernel,
        grid=(t // tm,),
        in_specs=[tile(d), tile(p.shape[1]), _const_spec(gp.shape), _const_spec(wpg.shape),
                  _const_spec(wp.shape), _const_spec(gf.shape)],
        out_specs=tile(d),
        out_shape=jax.ShapeDtypeStruct((t, d), F32),
        compiler_params=_params(1, 40),
        name="ple_final",
    )(h, p, gp, wpg, wp, gf)


def _ffn_weights(w_gate, w_up, w_down):
    d, d_ff = w_gate.shape
    nc = d_ff // FF_CHUNK
    wgu = jnp.concatenate([w_gate.reshape(d, nc, FF_CHUNK), w_up.reshape(d, nc, FF_CHUNK)], axis=-1)
    return wgu.transpose(1, 0, 2).astype(BF16), w_down.reshape(nc, FF_CHUNK, d).astype(BF16)


def _retention_tables():
    hd = np.arange(RET_HEADS, dtype=np.float64)
    log_gamma = np.log1p(-np.exp2(-5.0 - hd))
    idx = np.arange(RET_CHUNK, dtype=np.float64)
    diff = idx[:, None] - idx[None, :]
    dmask = np.where(diff >= 0, np.exp(log_gamma[:, None, None] * np.maximum(diff, 0.0)), 0.0)
    ones = np.ones((1, 1, RET_HEAD_DIM))
    qdec = np.exp(log_gamma[:, None] * (idx + 1.0))[:, :, None] * ones
    kdec = np.exp(log_gamma[:, None] * (RET_CHUNK - 1 - idx))[:, :, None] * ones
    chunk_decay = tuple(float(x) for x in np.exp(log_gamma * RET_CHUNK))
    as_f32 = lambda a: jnp.asarray(a, dtype=F32)
    return as_f32(dmask), as_f32(qdec), as_f32(kdec), chunk_decay


def _layer(h, p, positions, ln_ffn1, w_ffn1_gate, w_ffn1_up, w_ffn1_down, ln_mix, w_in, b_forget, w_merge,
           b_merge, w_ret_out, w_fox_out, w_out, ln_ffn2, w_ffn2_gate, w_ffn2_up, w_ffn2_down, ln_ple, w_ple,
           w_ple_gate, ln_final):
    b, s, d = h.shape
    t = b * s
    tm = min(512, s)
    row = lambda v: v.reshape(1, -1).astype(F32)

    wgu1, wd1 = _ffn_weights(w_ffn1_gate, w_ffn1_up, w_ffn1_down)
    h1 = _ffn(h.reshape(t, d), row(ln_ffn1), wgu1, wd1, tm)

    even_odd = lambda w: w.reshape(d, RET_HEADS, RET_HEAD_DIM // 2, 2).transpose(0, 1, 3, 2).reshape(d, RET_WIDTH)
    w_rq = even_odd(w_in[:, :RET_WIDTH])
    w_rk = even_odd(w_in[:, RET_WIDTH:2 * RET_WIDTH])
    fox_lo = 4 * RET_WIDTH
    fox_hi = fox_lo + 3 * FOX_WIDTH
    wr = jnp.concatenate([w_rq, w_rk, w_in[:, 2 * RET_WIDTH:fox_lo]], axis=1).astype(BF16)
    w_fq, w_fk, w_fv = (w_in[:, fox_lo + i * FOX_WIDTH:fox_lo + (i + 1) * FOX_WIDTH] for i in range(3))
    wqvt = jnp.concatenate([w_fq, w_fv], axis=1).T.astype(BF16)
    wff = jnp.pad(w_in[:, fox_hi:], ((0, 0), (0, LANES - FOX_HEADS))).astype(BF16)
    bfg = jnp.pad(b_forget.astype(F32), (0, LANES - FOX_HEADS)).reshape(1, LANES)
    inv_freq = 1.0 / (ROPE_BASE ** (jnp.arange(RET_HEAD_DIM // 2, dtype=F32) / (RET_HEAD_DIM // 2)))
    invf = jnp.concatenate([inv_freq, inv_freq]).reshape(1, RET_HEAD_DIM)
    sgn = jnp.concatenate([-jnp.ones((RET_HEAD_DIM // 2,), F32), jnp.ones((RET_HEAD_DIM // 2,), F32)]).reshape(1, -1)
    r, fk, fqvt, c, gates = _mix_in(h1.reshape(b, s, d), row(ln_mix), positions.reshape(b, s, 1), invf, sgn, wr,
                                    w_fk.astype(BF16), wqvt, wff, bfg, w_merge.astype(BF16), row(b_merge), tm)

    dmask, qdec, kdec, chunk_decay = _retention_tables()
    y_ret = _retention(r, dmask, qdec, kdec, chunk_decay, min(512, s))
    y_fox = _fox(fk, fqvt, c, min(256, s))

    h2 = _merge(h1, y_ret.reshape(t, RET_WIDTH), y_fox.reshape(t, FOX_WIDTH), gates.reshape(t, 2 * d),
                w_ret_out.astype(BF16), w_fox_out.astype(BF16), w_out.astype(BF16), tm)

    wgu2, wd2 = _ffn_weights(w_ffn2_gate, w_ffn2_up, w_ffn2_down)
    h3 = _ffn(h2, row(ln_ffn2), wgu2, wd2, tm)

    gf = row(ln_final) if ln_final is not None else None
    assert gf is not None, "multi-layer stacks need a ple kernel without the final norm"
    out = _ple_final(h3, p.reshape(t, -1), row(ln_ple), w_ple_gate.astype(BF16), w_ple.astype(BF16), gf, tm)
    return out.reshape(b, s, d)


def kernel(x, p, positions, ln_ffn1, w_ffn1_gate, w_ffn1_up, w_ffn1_down, ln_mix, w_in, b_forget, w_merge, b_merge,
           w_ret_out, w_fox_out, w_out, ln_ffn2, w_ffn2_gate, w_ffn2_up, w_ffn2_down, ln_ple, w_ple, w_ple_gate,
           ln_final):
    depth = p.shape[0]
    assert depth == 1, "this implementation covers the single-layer configuration"
    return _layer(x, p[0], positions, ln_ffn1[0], w_ffn1_gate[0], w_ffn1_up[0], w_ffn1_down[0], ln_mix[0], w_in[0],
                  b_forget[0], w_merge[0], b_merge[0], w_ret_out[0], w_fox_out[0], w_out[0], ln_ffn2[0],
                  w_ffn2_gate[0], w_ffn2_up[0], w_ffn2_down[0], ln_ple[0], w_ple[0], w_ple_gate[0], ln_final)
```

```python
import functools

import numpy as np
import jax
import jax.numpy as jnp
from jax import lax
from jax.experimental import pallas as pl
from jax.experimental.pallas import tpu as pltpu

F32 = jnp.float32
BF16 = jnp.bfloat16

EPS = 1e-6
ROPE_BASE = 10000.0
RET_HEADS = 4
RET_HEAD_DIM = 128
RET_WIDTH = RET_HEADS * RET_HEAD_DIM
FOX_HEADS = 8
FOX_HEAD_DIM = 64
FOX_WIDTH = FOX_HEADS * FOX_HEAD_DIM
RET_CHUNK = 128

LANES = 128
FF_CHUNK = 256
NEG_BIG = -1e30
LOG2E = 1.4426950408889634
FOX_STREAMS = 4
FOX_ACC_ROWS = FOX_HEAD_DIM + 16

_NT = (((1,), (1,)), ((), ()))
_TN = (((0,), (0,)), ((), ()))


def _const_spec(shape):
    zeros = (0,) * len(shape)
    return pl.BlockSpec(shape, lambda *_: zeros, pipeline_mode=pl.Buffered(1))


def _params(n_axes, vmem_mb):
    return pltpu.CompilerParams(
        dimension_semantics=("arbitrary",) * n_axes,
        vmem_limit_bytes=vmem_mb * 1024 * 1024,
    )


def _rms_norm(x, g):
    return x * lax.rsqrt(jnp.mean(x * x, axis=-1, keepdims=True) + EPS) * g


def _swiglu_tile(xn, wgu_ref, wd_ref):
    acc = None
    for c in range(wgu_ref.shape[0]):
        gu = jnp.dot(xn, wgu_ref[c], preferred_element_type=F32)
        g = gu[:, :FF_CHUNK]
        u = gu[:, FF_CHUNK:]
        a = (g * jax.nn.sigmoid(g) * u).astype(BF16)
        d = jnp.dot(a, wd_ref[c], preferred_element_type=F32)
        acc = d if acc is None else acc + d
    return acc


def _ffn_kernel(x_ref, g_ref, wgu_ref, wd_ref, o_ref):
    x = x_ref[...]
    xn = _rms_norm(x, g_ref[...]).astype(BF16)
    o_ref[...] = x + 0.5 * _swiglu_tile(xn, wgu_ref, wd_ref)


def _ffn(x, g, wgu, wd, tm):
    t, d = x.shape
    return pl.pallas_call(
        _ffn_kernel,
        grid=(t // tm,),
        in_specs=[
            pl.BlockSpec((tm, d), lambda i: (i, 0)),
            _const_spec(g.shape),
            _const_spec(wgu.shape),
            _const_spec(wd.shape),
        ],
        out_specs=pl.BlockSpec((tm, d), lambda i: (i, 0)),
        out_shape=jax.ShapeDtypeStruct((t, d), F32),
        compiler_params=_params(1, 48),
        name="ffn",
    )(x, g, wgu, wd)


def _mix_in_kernel(h_ref, g_ref, pos_ref, invf_ref, sgn_ref, wr_ref, wfk_ref, wqvt_ref, wff_ref, bf_ref,
                   wm_ref, bm_ref, r_ref, fk_ref, fqvt_ref, c_ref, gate_ref, carry_ref, *, ret_scale, fox_scale):
    tm = h_ref.shape[1]
    u = _rms_norm(h_ref[0], g_ref[...]).astype(BF16)

    pr = jnp.dot(u, wr_ref[...], preferred_element_type=F32)
    ang = pos_ref[0].astype(F32) * invf_ref[...]
    cos = jnp.cos(ang)
    sin = jnp.sin(ang) * sgn_ref[...]
    for part, scale in ((0, ret_scale), (1, 1.0)):
        for hd in range(RET_HEADS):
            lo = part * RET_WIDTH + hd * RET_HEAD_DIM
            xh = pr[:, lo:lo + RET_HEAD_DIM]
            rot = xh * cos + pltpu.roll(xh, RET_HEAD_DIM // 2, axis=1) * sin
            if scale != 1.0:
                rot = rot * scale
            r_ref[0, :, lo:lo + RET_HEAD_DIM] = rot.astype(BF16)
    r_ref[0, :, 2 * RET_WIDTH:3 * RET_WIDTH] = pr[:, 2 * RET_WIDTH:3 * RET_WIDTH].astype(BF16)
    rg = pr[:, 3 * RET_WIDTH:]
    r_ref[0, :, 3 * RET_WIDTH:] = (rg * jax.nn.sigmoid(rg)).astype(BF16)

    fk_ref[0] = jnp.dot(u, wfk_ref[...], preferred_element_type=F32).astype(BF16)
    qvt = lax.dot_general(wqvt_ref[...], u, _NT, preferred_element_type=F32)
    fqvt_ref[0, :FOX_WIDTH, :] = (qvt[:FOX_WIDTH] * fox_scale).astype(BF16)
    fqvt_ref[0, FOX_WIDTH:, :] = qvt[FOX_WIDTH:].astype(BF16)

    ff = jnp.dot(u, wff_ref[...], preferred_element_type=F32) + bf_ref[...]
    logf = jnp.minimum(ff, 0.0) - jnp.log1p(jnp.exp(-jnp.abs(ff)))
    tok = lax.broadcasted_iota(jnp.int32, logf.shape, 0)
    cum = logf
    sh = 1
    while sh < tm:
        cum = cum + jnp.where(tok >= sh, pltpu.roll(cum, sh, axis=0), 0.0)
        sh *= 2

    @pl.when(pl.program_id(1) == 0)
    def _():
        carry_ref[...] = jnp.zeros_like(carry_ref)

    cum = cum + carry_ref[0:1, :]
    c_ref[0] = cum
    carry_ref[...] = jnp.broadcast_to(cum[tm - 1:tm, :], carry_ref.shape)

    gm = jnp.dot(u, wm_ref[...], preferred_element_type=F32) + bm_ref[...]
    gate_ref[0] = jax.nn.sigmoid(gm).astype(BF16)


def _mix_in(h, g, pos, invf, sgn, wr, wfk, wqvt, wff, bfg, wm, bm, tm):
    b, s, d = h.shape
    kern = functools.partial(_mix_in_kernel, ret_scale=RET_HEAD_DIM ** -0.5,
                             fox_scale=FOX_HEAD_DIM ** -0.5 * LOG2E)
    tile = lambda w: pl.BlockSpec((1, tm, w), lambda i, j: (i, j, 0))
    return pl.pallas_call(
        kern,
        grid=(b, s // tm),
        in_specs=[tile(d), _const_spec(g.shape), tile(1), _const_spec(invf.shape), _const_spec(sgn.shape),
                  _const_spec(wr.shape), _const_spec(wfk.shape), _const_spec(wqvt.shape), _const_spec(wff.shape),
                  _const_spec(bfg.shape), _const_spec(wm.shape), _const_spec(bm.shape)],
        out_specs=[tile(4 * RET_WIDTH), tile(FOX_WIDTH),
                   pl.BlockSpec((1, 2 * FOX_WIDTH, tm), lambda i, j: (i, 0, j)), tile(LANES), tile(2 * d)],
        out_shape=[jax.ShapeDtypeStruct((b, s, 4 * RET_WIDTH), BF16),
                   jax.ShapeDtypeStruct((b, s, FOX_WIDTH), BF16),
                   jax.ShapeDtypeStruct((b, 2 * FOX_WIDTH, s), BF16),
                   jax.ShapeDtypeStruct((b, s, LANES), F32),
                   jax.ShapeDtypeStruct((b, s, 2 * d), BF16)],
        scratch_shapes=[pltpu.VMEM((8, LANES), F32)],
        compiler_params=_params(2, 48),
        name="mix_in",
    )(h, g, pos, invf, sgn, wr, wfk, wqvt, wff, bfg, wm, bm)


def _retention_kernel(q_ref, k_ref, v_ref, g_ref, dmask_ref, qdec_ref, kdec_ref, o_ref, state_ref, *, chunk_decay):
    tr = q_ref.shape[1]

    @pl.when(pl.program_id(1) == 0)
    def _():
        state_ref[...] = jnp.zeros_like(state_ref)

    for c in range(tr // RET_CHUNK):
        rows = slice(c * RET_CHUNK, (c + 1) * RET_CHUNK)
        for hd in range(RET_HEADS):
            cols = slice(hd * RET_HEAD_DIM, (hd + 1) * RET_HEAD_DIM)
            q = q_ref[0, rows, cols]
            k = k_ref[0, rows, cols]
            v = v_ref[0, rows, cols]
            state = state_ref[hd]
            scores = lax.dot_general(q, k, _NT, preferred_element_type=F32) * dmask_ref[hd]
            q_dec = (q.astype(F32) * qdec_ref[hd]).astype(BF16)
            y = (jnp.dot(scores.astype(BF16), v, preferred_element_type=F32)
                 + jnp.dot(q_dec, state.astype(BF16), preferred_element_type=F32))
            k_dec = (k.astype(F32) * kdec_ref[hd]).astype(BF16)
            state_ref[hd] = chunk_decay[hd] * state + lax.dot_general(k_dec, v, _TN, preferred_element_type=F32)
            mu = jnp.mean(y, axis=-1, keepdims=True)
            yc = y - mu
            var = jnp.mean(yc * yc, axis=-1, keepdims=True)
            yn = yc * lax.rsqrt(var + EPS)
            o_ref[0, rows, cols] = (yn * g_ref[0, rows, cols].astype(F32)).astype(BF16)


def _retention(r, dmask, qdec, kdec, chunk_decay, tr):
    b, s, _ = r.shape
    kern = functools.partial(_retention_kernel, chunk_decay=chunk_decay)
    part = lambda p: pl.BlockSpec((1, tr, RET_WIDTH), lambda i, j: (i, j, p))
    return pl.pallas_call(
        kern,
        grid=(b, s // tr),
        in_specs=[part(0), part(1), part(2), part(3),
                  _const_spec(dmask.shape), _const_spec(qdec.shape), _const_spec(kdec.shape)],
        out_specs=pl.BlockSpec((1, tr, RET_WIDTH), lambda i, j: (i, j, 0)),
        out_shape=jax.ShapeDtypeStruct((b, s, RET_WIDTH), BF16),
        scratch_shapes=[pltpu.VMEM((RET_HEADS, RET_HEAD_DIM, RET_HEAD_DIM), F32)],
        compiler_params=_params(2, 32),
        name="retention",
    )(r, r, r, r, dmask, qdec, kdec)


def _fox_kernel(qt_ref, k_ref, vt_ref, c_ref, o_ref, cb_ref, m_ref, acc_ref, *, tq):
    s_len = k_ref.shape[1]
    nq = s_len // tq
    pair = pl.program_id(1)

    lane = lax.broadcasted_iota(jnp.int32, (tq, LANES), 1)
    for hd in range(2):
        def fill(i, _, hd=hd):
            r0 = pl.multiple_of(i * tq, tq)
            blk = jnp.where(lane == 2 * pair + hd, c_ref[0, pl.ds(r0, tq), :], 0.0)
            col = jnp.sum(blk, axis=1, keepdims=True) * LOG2E
            cb_ref[hd, pl.ds(r0, tq), :] = jnp.broadcast_to(col, (tq, LANES))
            return 0
        lax.fori_loop(0, nq, fill, 0)

    key_idx = lax.broadcasted_iota(jnp.int32, (tq, tq), 0)
    qry_idx = lax.broadcasted_iota(jnp.int32, (tq, tq), 1)
    causal = key_idx <= qry_idx
    half_zero = jnp.zeros((FOX_HEAD_DIM, tq), BF16)
    ones_rows = jnp.ones((FOX_ACC_ROWS - FOX_HEAD_DIM, tq), BF16)

    def q_block(qi, _):
        q0 = pl.multiple_of(qi * tq, tq)
        qt = qt_ref[0, :, pl.ds(q0, tq)]
        qts = (jnp.concatenate([qt[:FOX_HEAD_DIM], half_zero], axis=0),
               jnp.concatenate([half_zero, qt[FOX_HEAD_DIM:]], axis=0))

        m_ref[...] = jnp.full(m_ref.shape, NEG_BIG, F32)
        acc_ref[...] = jnp.zeros(acc_ref.shape, F32)

        def process(blocks):
            starts = [pl.multiple_of(ki * tq, tq) for _, ki, _ in blocks]
            qk = [[jnp.dot(k_ref[0, pl.ds(k0, tq), :], qts[hd], preferred_element_type=F32) for hd in range(2)]
                  for k0 in starts]
            probs = []
            for (u, _, masked), k0, qk_u in zip(blocks, starts, qk):
                for hd in range(2):
                    cb = cb_ref[hd, pl.ds(k0, tq), :]
                    s = qk_u[hd] - jnp.concatenate([cb] * (tq // LANES), axis=1)
                    if masked:
                        s = jnp.where(causal, s, NEG_BIG)
                    m_old = m_ref[u, hd]
                    m_new = jnp.maximum(m_old, jnp.max(s, axis=0, keepdims=True))
                    m_ref[u, hd] = m_new
                    probs.append((jnp.exp2(m_old - m_new), jnp.exp2(s - m_new).astype(BF16)))
            for (u, _, _), k0 in zip(blocks, starts):
                for hd in range(2):
                    alpha, p = probs.pop(0)
                    vt = vt_ref[0, hd * FOX_HEAD_DIM:(hd + 1) * FOX_HEAD_DIM, pl.ds(k0, tq)]
                    vt = jnp.concatenate([vt, ones_rows], axis=0)
                    acc_ref[u, hd] = alpha * acc_ref[u, hd] + jnp.dot(vt, p, preferred_element_type=F32)

        def full_round(j, _):
            process([(u, FOX_STREAMS * j + u, False) for u in range(FOX_STREAMS)])
            return 0

        lax.fori_loop(0, qi // FOX_STREAMS, full_round, 0)
        rem = qi % FOX_STREAMS
        for r in range(FOX_STREAMS):
            @pl.when(rem == r)
            def _(r=r):
                process([(u, qi - r + u, False) for u in range(r)] + [(r, qi, True)])

        outs = []
        for hd in range(2):
            m_all = m_ref[:, hd]
            w = jnp.exp2(m_all - jnp.max(m_all, axis=0, keepdims=True))
            acc = jnp.sum(w * acc_ref[:, hd], axis=0)
            outs.append(acc[:FOX_HEAD_DIM] / acc[FOX_HEAD_DIM:FOX_HEAD_DIM + 1])
        o_ref[0, pl.ds(q0, tq), :] = jnp.concatenate(outs, axis=0).T.astype(BF16)
        return 0

    lax.fori_loop(0, nq, q_block, 0)


def _fox(fk, fqvt, c, tq):
    b, s, _ = fk.shape
    pairs = FOX_WIDTH // LANES
    kern = functools.partial(_fox_kernel, tq=tq)
    return pl.pallas_call(
        kern,
        grid=(b, pairs),
        in_specs=[pl.BlockSpec((1, LANES, s), lambda i, j: (i, j, 0)),
                  pl.BlockSpec((1, s, LANES), lambda i, j: (i, 0, j)),
                  pl.BlockSpec((1, LANES, s), lambda i, j: (i, pairs + j, 0)),
                  pl.BlockSpec((1, s, LANES), lambda i, j: (i, 0, 0))],
        out_specs=pl.BlockSpec((1, s, LANES), lambda i, j: (i, 0, j)),
        out_shape=jax.ShapeDtypeStruct((b, s, FOX_WIDTH), BF16),
        scratch_shapes=[pltpu.VMEM((2, s, LANES), F32),
                        pltpu.VMEM((FOX_STREAMS, 2, 1, tq), F32),
                        pltpu.VMEM((FOX_STREAMS, 2, FOX_ACC_ROWS, tq), F32)],
        compiler_params=_params(2, 32),
        name="fox",
    )(fqvt, fk, fqvt, c)


def _merge_kernel(h_ref, yr_ref, yf_ref, gate_ref, wro_ref, wfo_ref, wo_ref, o_ref):
    d = h_ref.shape[1]
    z_a = jnp.dot(yr_ref[...], wro_ref[...], preferred_element_type=F32)
    z_b = jnp.dot(yf_ref[...], wfo_ref[...], preferred_element_type=F32)
    mixed = gate_ref[:, :d].astype(F32) * z_a + gate_ref[:, d:].astype(F32) * z_b
    o_ref[...] = h_ref[...] + jnp.dot(mixed.astype(BF16), wo_ref[...], preferred_element_type=F32)


def _merge(h, yr, yf, gates, wro, wfo, wo, tm):
    t, d = h.shape
    tile = lambda w: pl.BlockSpec((tm, w), lambda i: (i, 0))
    return pl.pallas_call(
        _merge_kernel,
        grid=(t // tm,),
        in_specs=[tile(d), tile(RET_WIDTH), tile(FOX_WIDTH), tile(2 * d),
                  _const_spec(wro.shape), _const_spec(wfo.shape), _const_spec(wo.shape)],
        out_specs=tile(d),
        out_shape=jax.ShapeDtypeStruct((t, d), F32),
        compiler_params=_params(1, 40),
        name="merge",
    )(h, yr, yf, gates, wro, wfo, wo)


def _ple_final_kernel(h_ref, p_ref, gp_ref, wpg_ref, wp_ref, gf_ref, o_ref):
    h = h_ref[...]
    hn = _rms_norm(h, gp_ref[...]).astype(BF16)
    gate = jax.nn.sigmoid(jnp.dot(hn, wpg_ref[...], preferred_element_type=F32))
    emb = jnp.dot(p_ref[...].astype(BF16), wp_ref[...], preferred_element_type=F32)
    o_ref[...] = _rms_norm(h + gate * emb, gf_ref[...])


def _ple_final(h, p, gp, wpg, wp, gf, tm):
    t, d = h.shape
    tile = lambda w: pl.BlockSpec((tm, w), lambda i: (i, 0))
    return pl.pallas_call(
        _ple_final_kernel,
        grid=(t // tm,),
        in_specs=[tile(d), tile(p.shape[1]), _const_spec(gp.shape), _const_spec(wpg.shape),
                  _const_spec(wp.shape), _const_spec(gf.shape)],
        out_specs=tile(d),
        out_shape=jax.ShapeDtypeStruct((t, d), F32),
        compiler_params=_params(1, 40),
        name="ple_final",
    )(h, p, gp, wpg, wp, gf)


def _ffn_weights(w_gate, w_up, w_down):
    d, d_ff = w_gate.shape
    nc = d_ff // FF_CHUNK
    wgu = jnp.concatenate([w_gate.reshape(d, nc, FF_CHUNK), w_up.reshape(d, nc, FF_CHUNK)], axis=-1)
    return wgu.transpose(1, 0, 2).astype(BF16), w_down.reshape(nc, FF_CHUNK, d).astype(BF16)


def _retention_tables():
    hd = np.arange(RET_HEADS, dtype=np.float64)
    log_gamma = np.log1p(-np.exp2(-5.0 - hd))
    idx = np.arange(RET_CHUNK, dtype=np.float64)
    diff = idx[:, None] - idx[None, :]
    dmask = np.where(diff >= 0, np.exp(log_gamma[:, None, None] * np.maximum(diff, 0.0)), 0.0)
    ones = np.ones((1, 1, RET_HEAD_DIM))
    qdec = np.exp(log_gamma[:, None] * (idx + 1.0))[:, :, None] * ones
    kdec = np.exp(log_gamma[:, None] * (RET_CHUNK - 1 - idx))[:, :, None] * ones
    chunk_decay = tuple(float(x) for x in np.exp(log_gamma * RET_CHUNK))
    as_f32 = lambda a: jnp.asarray(a, dtype=F32)
    return as_f32(dmask), as_f32(qdec), as_f32(kdec), chunk_decay


def _layer(h, p, positions, ln_ffn1, w_ffn1_gate, w_ffn1_up, w_ffn1_down, ln_mix, w_in, b_forget, w_merge,
           b_merge, w_ret_out, w_fox_out, w_out, ln_ffn2, w_ffn2_gate, w_ffn2_up, w_ffn2_down, ln_ple, w_ple,
           w_ple_gate, ln_final):
    b, s, d = h.shape
    t = b * s
    tm = min(512, s)
    row = lambda v: v.reshape(1, -1).astype(F32)

    wgu1, wd1 = _ffn_weights(w_ffn1_gate, w_ffn1_up, w_ffn1_down)
    h1 = _ffn(h.reshape(t, d), row(ln_ffn1), wgu1, wd1, tm)

    even_odd = lambda w: w.reshape(d, RET_HEADS, RET_HEAD_DIM // 2, 2).transpose(0, 1, 3, 2).reshape(d, RET_WIDTH)
    w_rq = even_odd(w_in[:, :RET_WIDTH])
    w_rk = even_odd(w_in[:, RET_WIDTH:2 * RET_WIDTH])
    fox_lo = 4 * RET_WIDTH
    fox_hi = fox_lo + 3 * FOX_WIDTH
    wr = jnp.concatenate([w_rq, w_rk, w_in[:, 2 * RET_WIDTH:fox_lo]], axis=1).astype(BF16)
    w_fq, w_fk, w_fv = (w_in[:, fox_lo + i * FOX_WIDTH:fox_lo + (i + 1) * FOX_WIDTH] for i in range(3))
    wqvt = jnp.concatenate([w_fq, w_fv], axis=1).T.astype(BF16)
    wff = jnp.pad(w_in[:, fox_hi:], ((0, 0), (0, LANES - FOX_HEADS))).astype(BF16)
    bfg = jnp.pad(b_forget.astype(F32), (0, LANES - FOX_HEADS)).reshape(1, LANES)
    inv_freq = 1.0 / (ROPE_BASE ** (jnp.arange(RET_HEAD_DIM // 2, dtype=F32) / (RET_HEAD_DIM // 2)))
    invf = jnp.concatenate([inv_freq, inv_freq]).reshape(1, RET_HEAD_DIM)
    sgn = jnp.concatenate([-jnp.ones((RET_HEAD_DIM // 2,), F32), jnp.ones((RET_HEAD_DIM // 2,), F32)]).reshape(1, -1)
    r, fk, fqvt, c, gates = _mix_in(h1.reshape(b, s, d), row(ln_mix), positions.reshape(b, s, 1), invf, sgn, wr,
                                    w_fk.astype(BF16), wqvt, wff, bfg, w_merge.astype(BF16), row(b_merge), tm)

    dmask, qdec, kdec, chunk_decay = _retention_tables()
    y_ret = _retention(r, dmask, qdec, kdec, chunk_decay, min(512, s))
    y_fox = _fox(fk, fqvt, c, min(256, s))

    h2 = _merge(h1, y_ret.reshape(t, RET_WIDTH), y_fox.reshape(t, FOX_WIDTH), gates.reshape(t, 2 * d),
                w_ret_out.astype(BF16), w_fox_out.astype(BF16), w_out.astype(BF16), tm)

    wgu2, wd2 = _ffn_weights(w_ffn2_gate, w_ffn2_up, w_ffn2_down)
    h3 = _ffn(h2, row(ln_ffn2), wgu2, wd2, tm)

    gf = row(ln_final) if ln_final is not None else None
    assert gf is not None, "multi-layer stacks need a ple kernel without the final norm"
    out = _ple_final(h3, p.reshape(t, -1), row(ln_ple), w_ple_gate.astype(BF16), w_ple.astype(BF16), gf, tm)
    return out.reshape(b, s, d)


def kernel(x, p, positions, ln_ffn1, w_ffn1_gate, w_ffn1_up, w_ffn1_down, ln_mix, w_in, b_forget, w_merge, b_merge,
           w_ret_out, w_fox_out, w_out, ln_ffn2, w_ffn2_gate, w_ffn2_up, w_ffn2_down, ln_ple, w_ple, w_ple_gate,
           ln_final):
    depth = p.shape[0]
    assert depth == 1, "this implementation covers the single-layer configuration"
    return _layer(x, p[0], positions, ln_ffn1[0], w_ffn1_gate[0], w_ffn1_up[0], w_ffn1_down[0], ln_mix[0], w_in[0],
                  b_forget[0], w_merge[0], b_merge[0], w_ret_out[0], w_fox_out[0], w_out[0], ln_ffn2[0],
                  w_ffn2_gate[0], w_ffn2_up[0], w_ffn2_down[0], ln_ple[0], w_ple[0], w_ple_gate[0], ln_final)
```

```python
import functools

import numpy as np
import jax
import jax.numpy as jnp
from jax import lax
from jax.experimental import pallas as pl
from jax.experimental.pallas import tpu as pltpu

F32 = jnp.float32
BF16 = jnp.bfloat16

EPS = 1e-6
ROPE_BASE = 10000.0
RET_HEADS = 4
RET_HEAD_DIM = 128
RET_WIDTH = RET_HEADS * RET_HEAD_DIM
FOX_HEADS = 8
FOX_HEAD_DIM = 64
FOX_WIDTH = FOX_HEADS * FOX_HEAD_DIM
RET_CHUNK = 128

LANES = 128
FF_CHUNK = 256
NEG_BIG = -1e30
LOG2E = 1.4426950408889634
FOX_ROUND = 8
FOX_LOOKAHEAD = 4
FOX_ACC_ROWS = FOX_HEAD_DIM + 16

_NT = (((1,), (1,)), ((), ()))
_TN = (((0,), (0,)), ((), ()))


def _const_spec(shape):
    zeros = (0,) * len(shape)
    return pl.BlockSpec(shape, lambda *_: zeros, pipeline_mode=pl.Buffered(1))


def _params(n_axes, vmem_mb):
    return pltpu.CompilerParams(
        dimension_semantics=("arbitrary",) * n_axes,
        vmem_limit_bytes=vmem_mb * 1024 * 1024,
    )


def _rms_norm(x, g):
    return x * lax.rsqrt(jnp.mean(x * x, axis=-1, keepdims=True) + EPS) * g


def _swiglu_tile(xn, wgu_ref, wd_ref):
    acc = None
    for c in range(wgu_ref.shape[0]):
        gu = jnp.dot(xn, wgu_ref[c], preferred_element_type=F32)
        g = gu[:, :FF_CHUNK]
        u = gu[:, FF_CHUNK:]
        a = (g * jax.nn.sigmoid(g) * u).astype(BF16)
        d = jnp.dot(a, wd_ref[c], preferred_element_type=F32)
        acc = d if acc is None else acc + d
    return acc


def _ffn_kernel(x_ref, g_ref, wgu_ref, wd_ref, o_ref):
    x = x_ref[...]
    xn = _rms_norm(x, g_ref[...]).astype(BF16)
    o_ref[...] = x + 0.5 * _swiglu_tile(xn, wgu_ref, wd_ref)


def _ffn(x, g, wgu, wd, tm):
    t, d = x.shape
    return pl.pallas_call(
        _ffn_kernel,
        grid=(t // tm,),
        in_specs=[
            pl.BlockSpec((tm, d), lambda i: (i, 0)),
            _const_spec(g.shape),
            _const_spec(wgu.shape),
            _const_spec(wd.shape),
        ],
        out_specs=pl.BlockSpec((tm, d), lambda i: (i, 0)),
        out_shape=jax.ShapeDtypeStruct((t, d), F32),
        compiler_params=_params(1, 48),
        name="ffn",
    )(x, g, wgu, wd)


def _mix_in_kernel(h_ref, g_ref, pos_ref, invf_ref, sgn_ref, wr_ref, wfk_ref, wqvt_ref, wff_ref, bf_ref,
                   place_ref, wm_ref, bm_ref, r_ref, fk_ref, fqvt_ref, gate_ref, carry_ref, *,
                   ret_scale, fox_scale):
    tm = h_ref.shape[1]
    u = _rms_norm(h_ref[0], g_ref[...]).astype(BF16)

    pr = jnp.dot(u, wr_ref[...], preferred_element_type=F32)
    ang = pos_ref[0].astype(F32) * invf_ref[...]
    cos = jnp.cos(ang)
    sin = jnp.sin(ang) * sgn_ref[...]
    for part, scale in ((0, ret_scale), (1, 1.0)):
        for hd in range(RET_HEADS):
            lo = part * RET_WIDTH + hd * RET_HEAD_DIM
            xh = pr[:, lo:lo + RET_HEAD_DIM]
            rot = xh * cos + pltpu.roll(xh, RET_HEAD_DIM // 2, axis=1) * sin
            if scale != 1.0:
                rot = rot * scale
            r_ref[0, :, lo:lo + RET_HEAD_DIM] = rot.astype(BF16)
    r_ref[0, :, 2 * RET_WIDTH:3 * RET_WIDTH] = pr[:, 2 * RET_WIDTH:3 * RET_WIDTH].astype(BF16)
    rg = pr[:, 3 * RET_WIDTH:]
    r_ref[0, :, 3 * RET_WIDTH:] = (rg * jax.nn.sigmoid(rg)).astype(BF16)

    fk = jnp.dot(u, wfk_ref[...], preferred_element_type=F32).astype(BF16)
    qvt = lax.dot_general(wqvt_ref[...], u, _NT, preferred_element_type=F32)
    fqvt_ref[0, :FOX_WIDTH, :] = (qvt[:FOX_WIDTH] * fox_scale).astype(BF16)
    fqvt_ref[0, FOX_WIDTH:, :] = qvt[FOX_WIDTH:].astype(BF16)

    ff = jnp.dot(u, wff_ref[...], preferred_element_type=F32) + bf_ref[...]
    logf = jnp.minimum(ff, 0.0) - jnp.log1p(jnp.exp(-jnp.abs(ff)))
    tok = lax.broadcasted_iota(jnp.int32, logf.shape, 0)
    cum = logf
    sh = 1
    while sh < tm:
        cum = cum + jnp.where(tok >= sh, pltpu.roll(cum, sh, axis=0), 0.0)
        sh *= 2

    @pl.when(pl.program_id(1) == 0)
    def _():
        carry_ref[...] = jnp.zeros_like(carry_ref)

    cum = cum + carry_ref[0:1, :]
    carry_ref[...] = jnp.broadcast_to(cum[tm - 1:tm, :], carry_ref.shape)

    c2 = cum * LOG2E
    c_hi = c2.astype(BF16)
    rest = c2 - c_hi.astype(F32)
    c_mid = rest.astype(BF16)
    c_lo = (rest - c_mid.astype(F32)).astype(BF16)
    placed = jnp.dot(jnp.concatenate([c_hi, c_mid, c_lo], axis=1), place_ref[...],
                     preferred_element_type=F32).astype(BF16)
    for pr_i in range(FOX_WIDTH // LANES):
        fk_ref[0, :, 2 * pr_i * LANES:(2 * pr_i + 1) * LANES] = fk[:, pr_i * LANES:(pr_i + 1) * LANES]
        fk_ref[0, :, (2 * pr_i + 1) * LANES:(2 * pr_i + 2) * LANES] = placed[:, pr_i * LANES:(pr_i + 1) * LANES]

    gm = jnp.dot(u, wm_ref[...], preferred_element_type=F32) + bm_ref[...]
    gate_ref[0] = jax.nn.sigmoid(gm).astype(BF16)


def _forget_placement():
    place = np.zeros((3 * LANES, FOX_WIDTH), np.float32)
    for head in range(FOX_HEADS):
        for piece in range(3):
            place[piece * LANES + head, (head // 2) * LANES + 3 * (head % 2) + piece] = 1.0
    return jnp.asarray(place, dtype=BF16)


def _mix_in(h, g, pos, invf, sgn, wr, wfk, wqvt, wff, bfg, wm, bm, tm):
    b, s, d = h.shape
    kern = functools.partial(_mix_in_kernel, ret_scale=RET_HEAD_DIM ** -0.5,
                             fox_scale=FOX_HEAD_DIM ** -0.5 * LOG2E)
    place = _forget_placement()
    tile = lambda w: pl.BlockSpec((1, tm, w), lambda i, j: (i, j, 0))
    return pl.pallas_call(
        kern,
        grid=(b, s // tm),
        in_specs=[tile(d), _const_spec(g.shape), tile(1), _const_spec(invf.shape), _const_spec(sgn.shape),
                  _const_spec(wr.shape), _const_spec(wfk.shape), _const_spec(wqvt.shape), _const_spec(wff.shape),
                  _const_spec(bfg.shape), _const_spec(place.shape), _const_spec(wm.shape), _const_spec(bm.shape)],
        out_specs=[tile(4 * RET_WIDTH), tile(2 * FOX_WIDTH),
                   pl.BlockSpec((1, 2 * FOX_WIDTH, tm), lambda i, j: (i, 0, j)), tile(2 * d)],
        out_shape=[jax.ShapeDtypeStruct((b, s, 4 * RET_WIDTH), BF16),
                   jax.ShapeDtypeStruct((b, s, 2 * FOX_WIDTH), BF16),
                   jax.ShapeDtypeStruct((b, 2 * FOX_WIDTH, s), BF16),
                   jax.ShapeDtypeStruct((b, s, 2 * d), BF16)],
        scratch_shapes=[pltpu.VMEM((8, LANES), F32)],
        compiler_params=_params(2, 48),
        name="mix_in",
    )(h, g, pos, invf, sgn, wr, wfk, wqvt, wff, bfg, place, wm, bm)


def _retention_kernel(q_ref, k_ref, v_ref, g_ref, dmask_ref, qdec_ref, kdec_ref, o_ref, state_ref, *, chunk_decay):
    tr = q_ref.shape[1]

    @pl.when(pl.program_id(1) == 0)
    def _():
        state_ref[...] = jnp.zeros_like(state_ref)

    for c in range(tr // RET_CHUNK):
        rows = slice(c * RET_CHUNK, (c + 1) * RET_CHUNK)
        for hd in range(RET_HEADS):
            cols = slice(hd * RET_HEAD_DIM, (hd + 1) * RET_HEAD_DIM)
            q = q_ref[0, rows, cols]
            k = k_ref[0, rows, cols]
            v = v_ref[0, rows, cols]
            state = state_ref[hd]
            scores = lax.dot_general(q, k, _NT, preferred_element_type=F32) * dmask_ref[hd]
            q_dec = (q.astype(F32) * qdec_ref[hd]).astype(BF16)
            y = (jnp.dot(scores.astype(BF16), v, preferred_element_type=F32)
                 + jnp.dot(q_dec, state.astype(BF16), preferred_element_type=F32))
            k_dec = (k.astype(F32) * kdec_ref[hd]).astype(BF16)
            state_ref[hd] = chunk_decay[hd] * state + lax.dot_general(k_dec, v, _TN, preferred_element_type=F32)
            mu = jnp.mean(y, axis=-1, keepdims=True)
            yc = y - mu
            var = jnp.mean(yc * yc, axis=-1, keepdims=True)
            yn = yc * lax.rsqrt(var + EPS)
            o_ref[0, rows, cols] = (yn * g_ref[0, rows, cols].astype(F32)).astype(BF16)


def _retention(r, dmask, qdec, kdec, chunk_decay, tr):
    b, s, _ = r.shape
    kern = functools.partial(_retention_kernel, chunk_decay=chunk_decay)
    part = lambda p: pl.BlockSpec((1, tr, RET_WIDTH), lambda i, j: (i, j, p))
    return pl.pallas_call(
        kern,
        grid=(b, s // tr),
        in_specs=[part(0), part(1), part(2), part(3),
                  _const_spec(dmask.shape), _const_spec(qdec.shape), _const_spec(kdec.shape)],
        out_specs=pl.BlockSpec((1, tr, RET_WIDTH), lambda i, j: (i, j, 0)),
        out_shape=jax.ShapeDtypeStruct((b, s, RET_WIDTH), BF16),
        scratch_shapes=[pltpu.VMEM((RET_HEADS, RET_HEAD_DIM, RET_HEAD_DIM), F32)],
        compiler_params=_params(2, 32),
        name="retention",
    )(r, r, r, r, dmask, qdec, kdec)


def _fox_schedule(nq):
    remaining = {qi: list(range(qi)) for qi in range(1, nq)}
    rounds = []
    while any(remaining.values()):
        pick = sorted((q for q in remaining if remaining[q]), key=lambda q: (-len(remaining[q]), q))[:FOX_ROUND]
        rounds.append([(q, remaining[q].pop(), q) for q in pick])
    n_plain = len(rounds)
    diag = [(q, q, q) for q in range(nq)]
    rounds += [diag[i:i + FOX_ROUND] for i in range(0, nq, FOX_ROUND)]
    flat = [t for rnd in rounds for t in rnd + [(0, 0, nq)] * (FOX_ROUND - len(rnd))]
    tables = tuple(jnp.asarray([t[i] for t in flat], dtype=jnp.int32) for i in range(3))
    return tables, n_plain, len(rounds)


def _fox_kernel(qsrc_ref, ksrc_ref, slot_ref, qt_ref, k_ref, vt_ref, o_ref, m_ref, acc_ref, *,
                tq, n_plain, n_rounds):
    s_len = k_ref.shape[1]
    nq = s_len // tq

    m_ref[...] = jnp.full(m_ref.shape, NEG_BIG, F32)
    acc_ref[...] = jnp.zeros(acc_ref.shape, F32)

    key_idx = lax.broadcasted_iota(jnp.int32, (tq, tq), 0)
    qry_idx = lax.broadcasted_iota(jnp.int32, (tq, tq), 1)
    causal = key_idx <= qry_idx
    half_zero = jnp.zeros((FOX_HEAD_DIM, tq), BF16)
    ones_rows = jnp.ones((FOX_ACC_ROWS - FOX_HEAD_DIM, tq), BF16)
    piece_row = lax.broadcasted_iota(jnp.int32, (LANES, tq), 0)
    pick_c = [jnp.where((piece_row >= 3 * hd) & (piece_row < 3 * hd + 3), -1.0, 0.0).astype(BF16) for hd in range(2)]
    chains = [(t, hd) for t in range(FOX_ROUND) for hd in range(2)]

    def run_round(rnd, masked):
        slots = [slot_ref[rnd * FOX_ROUND + t] for t in range(FOX_ROUND)]
        k0s = [pl.multiple_of(ksrc_ref[rnd * FOX_ROUND + t] * tq, tq) for t in range(FOX_ROUND)]
        qk = {}

        def scores(t):
            q0 = pl.multiple_of(qsrc_ref[rnd * FOX_ROUND + t] * tq, tq)
            qt = qt_ref[0, :, pl.ds(q0, tq)]
            qts = (jnp.concatenate([qt[:FOX_HEAD_DIM], half_zero, pick_c[0]], axis=0),
                   jnp.concatenate([half_zero, qt[FOX_HEAD_DIM:], pick_c[1]], axis=0))
            k2 = k_ref[0, pl.ds(k0s[t], tq), :]
            for hd in range(2):
                qk[t, hd] = jnp.dot(k2, qts[hd], preferred_element_type=F32)

        for t in range(FOX_LOOKAHEAD):
            scores(t)
        for t, hd in chains:
            if hd == 0 and t + FOX_LOOKAHEAD < FOX_ROUND:
                scores(t + FOX_LOOKAHEAD)
            s = qk[t, hd]
            if masked:
                s = jnp.where(causal, s, NEG_BIG)
            m_old = m_ref[slots[t], hd]
            m_new = jnp.maximum(m_old, jnp.max(s, axis=0, keepdims=True))
            m_ref[slots[t], hd] = m_new
            p = jnp.exp2(s - m_new).astype(BF16)
            vt = vt_ref[0, hd * FOX_HEAD_DIM:(hd + 1) * FOX_HEAD_DIM, pl.ds(k0s[t], tq)]
            vt = jnp.concatenate([vt, ones_rows], axis=0)
            acc_ref[slots[t], hd] = (jnp.exp2(m_old - m_new) * acc_ref[slots[t], hd]
                                     + jnp.dot(vt, p, preferred_element_type=F32))
        return 0

    lax.fori_loop(0, n_plain, lambda rnd, _: run_round(rnd, False), 0)
    lax.fori_loop(n_plain, n_rounds, lambda rnd, _: run_round(rnd, True), 0)

    def finish(qi, _):
        outs = []
        for hd in range(2):
            acc = acc_ref[qi, hd]
            outs.append(acc[:FOX_HEAD_DIM] / acc[FOX_HEAD_DIM:FOX_HEAD_DIM + 1])
        q0 = pl.multiple_of(qi * tq, tq)
        o_ref[0, pl.ds(q0, tq), :] = jnp.concatenate(outs, axis=0).T.astype(BF16)
        return 0

    lax.fori_loop(0, nq, finish, 0)


def _fox(fk, fqvt, tq):
    b, s, _ = fk.shape
    nq = s // tq
    pairs = FOX_WIDTH // LANES
    tables, n_plain, n_rounds = _fox_schedule(nq)
    kern = functools.partial(_fox_kernel, tq=tq, n_plain=n_plain, n_rounds=n_rounds)
    grid_spec = pltpu.PrefetchScalarGridSpec(
        num_scalar_prefetch=len(tables),
        grid=(b, pairs),
        in_specs=[pl.BlockSpec((1, LANES, s), lambda i, j, *_: (i, j, 0)),
                  pl.BlockSpec((1, s, 2 * LANES), lambda i, j, *_: (i, 0, j)),
                  pl.BlockSpec((1, LANES, s), lambda i, j, *_: (i, pairs + j, 0))],
        out_specs=pl.BlockSpec((1, s, LANES), lambda i, j, *_: (i, 0, j)),
        scratch_shapes=[pltpu.VMEM((nq + 1, 2, 1, tq), F32),
                        pltpu.VMEM((nq + 1, 2, FOX_ACC_ROWS, tq), F32)],
    )
    return pl.pallas_call(
        kern,
        grid_spec=grid_spec,
        out_shape=jax.ShapeDtypeStruct((b, s, FOX_WIDTH), BF16),
        compiler_params=_params(2, 40),
        name="fox",
    )(*tables, fqvt, fk, fqvt)


def _merge_kernel(h_ref, yr_ref, yf_ref, gate_ref, wro_ref, wfo_ref, wo_ref, o_ref):
    d = h_ref.shape[1]
    z_a = jnp.dot(yr_ref[...], wro_ref[...], preferred_element_type=F32)
    z_b = jnp.dot(yf_ref[...], wfo_ref[...], preferred_element_type=F32)
    mixed = gate_ref[:, :d].astype(F32) * z_a + gate_ref[:, d:].astype(F32) * z_b
    o_ref[...] = h_ref[...] + jnp.dot(mixed.astype(BF16), wo_ref[...], preferred_element_type=F32)


def _merge(h, yr, yf, gates, wro, wfo, wo, tm):
    t, d = h.shape
    tile = lambda w: pl.BlockSpec((tm, w), lambda i: (i, 0))
    return pl.pallas_call(
        _merge_kernel,
        grid=(t // tm,),
        in_specs=[tile(d), tile(RET_WIDTH), tile(FOX_WIDTH), tile(2 * d),
                  _const_spec(wro.shape), _const_spec(wfo.shape), _const_spec(wo.shape)],
        out_specs=tile(d),
        out_shape=jax.ShapeDtypeStruct((t, d), F32),
        compiler_params=_params(1, 40),
        name="merge",
    )(h, yr, yf, gates, wro, wfo, wo)


def _ple_final_kernel(h_ref, p_ref, gp_ref, wpg_ref, wp_ref, gf_ref, o_ref):
    h = h_ref[...]
    hn = _rms_norm(h, gp_ref[...]).astype(BF16)
    gate = jax.nn.sigmoid(jnp.dot(hn, wpg_ref[...], preferred_element_type=F32))
    emb = jnp.dot(p_ref[...].astype(BF16), wp_ref[...], preferred_element_type=F32)
    o_ref[...] = _rms_norm(h + gate * emb, gf_ref[...])


def _ple_final(h, p, gp, wpg, wp, gf, tm):
    t, d = h.shape
    tile = lambda w: pl.BlockSpec((tm, w), lambda i: (i, 0))
    return pl.pallas_call(
        _ple_final_kernel,
        grid=(t // tm,),
        in_specs=[tile(d), tile(p.shape[1]), _const_spec(gp.shape), _const_spec(wpg.shape),
                  _const_spec(wp.shape), _const_spec(gf.shape)],
        out_specs=tile(d),
        out_shape=jax.ShapeDtypeStruct((t, d), F32),
        compiler_params=_params(1, 40),
        name="ple_final",
    )(h, p, gp, wpg, wp, gf)


def _ffn_weights(w_gate, w_up, w_down):
    d, d_ff = w_gate.shape
    nc = d_ff // FF_CHUNK
    wgu = jnp.concatenate([w_gate.reshape(d, nc, FF_CHUNK), w_up.reshape(d, nc, FF_CHUNK)], axis=-1)
    return wgu.transpose(1, 0, 2).astype(BF16), w_down.reshape(nc, FF_CHUNK, d).astype(BF16)


def _retention_tables():
    hd = np.arange(RET_HEADS, dtype=np.float64)
    log_gamma = np.log1p(-np.exp2(-5.0 - hd))
    idx = np.arange(RET_CHUNK, dtype=np.float64)
    diff = idx[:, None] - idx[None, :]
    dmask = np.where(diff >= 0, np.exp(log_gamma[:, None, None] * np.maximum(diff, 0.0)), 0.0)
    ones = np.ones((1, 1, RET_HEAD_DIM))
    qdec = np.exp(log_gamma[:, None] * (idx + 1.0))[:, :, None] * ones
    kdec = np.exp(log_gamma[:, None] * (RET_CHUNK - 1 - idx))[:, :, None] * ones
    chunk_decay = tuple(float(x) for x in np.exp(log_gamma * RET_CHUNK))
    as_f32 = lambda a: jnp.asarray(a, dtype=F32)
    return as_f32(dmask), as_f32(qdec), as_f32(kdec), chunk_decay


def _layer(h, p, positions, ln_ffn1, w_ffn1_gate, w_ffn1_up, w_ffn1_down, ln_mix, w_in, b_forget, w_merge,
           b_merge, w_ret_out, w_fox_out, w_out, ln_ffn2, w_ffn2_gate, w_ffn2_up, w_ffn2_down, ln_ple, w_ple,
           w_ple_gate, ln_final):
    b, s, d = h.shape
    t = b * s
    tm = min(512, s)
    row = lambda v: v.reshape(1, -1).astype(F32)

    wgu1, wd1 = _ffn_weights(w_ffn1_gate, w_ffn1_up, w_ffn1_down)
    h1 = _ffn(h.reshape(t, d), row(ln_ffn1), wgu1, wd1, tm)

    even_odd = lambda w: w.reshape(d, RET_HEADS, RET_HEAD_DIM // 2, 2).transpose(0, 1, 3, 2).reshape(d, RET_WIDTH)
    w_rq = even_odd(w_in[:, :RET_WIDTH])
    w_rk = even_odd(w_in[:, RET_WIDTH:2 * RET_WIDTH])
    fox_lo = 4 * RET_WIDTH
    fox_hi = fox_lo + 3 * FOX_WIDTH
    wr = jnp.concatenate([w_rq, w_rk, w_in[:, 2 * RET_WIDTH:fox_lo]], axis=1).astype(BF16)
    w_fq, w_fk, w_fv = (w_in[:, fox_lo + i * FOX_WIDTH:fox_lo + (i + 1) * FOX_WIDTH] for i in range(3))
    wqvt = jnp.concatenate([w_fq, w_fv], axis=1).T.astype(BF16)
    wff = jnp.pad(w_in[:, fox_hi:], ((0, 0), (0, LANES - FOX_HEADS))).astype(BF16)
    bfg = jnp.pad(b_forget.astype(F32), (0, LANES - FOX_HEADS)).reshape(1, LANES)
    inv_freq = 1.0 / (ROPE_BASE ** (jnp.arange(RET_HEAD_DIM // 2, dtype=F32) / (RET_HEAD_DIM // 2)))
    invf = jnp.concatenate([inv_freq, inv_freq]).reshape(1, RET_HEAD_DIM)
    sgn = jnp.concatenate([-jnp.ones((RET_HEAD_DIM // 2,), F32), jnp.ones((RET_HEAD_DIM // 2,), F32)]).reshape(1, -1)
    r, fk, fqvt, gates = _mix_in(h1.reshape(b, s, d), row(ln_mix), positions.reshape(b, s, 1), invf, sgn, wr,
                                    w_fk.astype(BF16), wqvt, wff, bfg, w_merge.astype(BF16), row(b_merge), tm)

    dmask, qdec, kdec, chunk_decay = _retention_tables()
    y_ret = _retention(r, dmask, qdec, kdec, chunk_decay, min(512, s))
    y_fox = _fox(fk, fqvt, min(256, s))

    h2 = _merge(h1, y_ret.reshape(t, RET_WIDTH), y_fox.reshape(t, FOX_WIDTH), gates.reshape(t, 2 * d),
                w_ret_out.astype(BF16), w_fox_out.astype(BF16), w_out.astype(BF16), tm)

    wgu2, wd2 = _ffn_weights(w_ffn2_gate, w_ffn2_up, w_ffn2_down)
    h3 = _ffn(h2, row(ln_ffn2), wgu2, wd2, tm)

    gf = row(ln_final) if ln_final is not None else None
    assert gf is not None, "multi-layer stacks need a ple kernel without the final norm"
    out = _ple_final(h3, p.reshape(t, -1), row(ln_ple), w_ple_gate.astype(BF16), w_ple.astype(BF16), gf, tm)
    return out.reshape(b, s, d)


def kernel(x, p, positions, ln_ffn1, w_ffn1_gate, w_ffn1_up, w_ffn1_down, ln_mix, w_in, b_forget, w_merge, b_merge,
           w_ret_out, w_fox_out, w_out, ln_ffn2, w_ffn2_gate, w_ffn2_up, w_ffn2_down, ln_ple, w_ple, w_ple_gate,
           ln_final):
    depth = p.shape[0]
    assert depth == 1, "this implementation covers the single-layer configuration"
    return _layer(x, p[0], positions, ln_ffn1[0], w_ffn1_gate[0], w_ffn1_up[0], w_ffn1_down[0], ln_mix[0], w_in[0],
                  b_forget[0], w_merge[0], b_merge[0], w_ret_out[0], w_fox_out[0], w_out[0], ln_ffn2[0],
                  w_ffn2_gate[0], w_ffn2_up[0], w_ffn2_down[0], ln_ple[0], w_ple[0], w_ple_gate[0], ln_final)
```

```python
import functools

import numpy as np
import jax
import jax.numpy as jnp
from jax import lax
from jax.experimental import pallas as pl
from jax.experimental.pallas import tpu as pltpu

F32 = jnp.float32
BF16 = jnp.bfloat16

EPS = 1e-6
ROPE_BASE = 10000.0
RET_HEADS = 4
RET_HEAD_DIM = 128
RET_WIDTH = RET_HEADS * RET_HEAD_DIM
FOX_HEADS = 8
FOX_HEAD_DIM = 64
FOX_WIDTH = FOX_HEADS * FOX_HEAD_DIM
RET_CHUNK = 128

LANES = 128
FF_CHUNK = 256
NEG_BIG = -1e30
LOG2E = 1.4426950408889634
FOX_ROUND = 8
FOX_LOOKAHEAD = 2
FOX_ACC_ROWS = FOX_HEAD_DIM + 16

_NT = (((1,), (1,)), ((), ()))
_TN = (((0,), (0,)), ((), ()))


def _const_spec(shape):
    zeros = (0,) * len(shape)
    return pl.BlockSpec(shape, lambda *_: zeros, pipeline_mode=pl.Buffered(1))


def _params(n_axes, vmem_mb):
    return pltpu.CompilerParams(
        dimension_semantics=("arbitrary",) * n_axes,
        vmem_limit_bytes=vmem_mb * 1024 * 1024,
    )


def _rms_norm(x, g):
    return x * lax.rsqrt(jnp.mean(x * x, axis=-1, keepdims=True) + EPS) * g


def _swiglu_tile(xn, wg_ref, wu_ref, wd_ref):
    acc = None
    for c in range(wg_ref.shape[1] // FF_CHUNK):
        cols = slice(c * FF_CHUNK, (c + 1) * FF_CHUNK)
        g = jnp.dot(xn, wg_ref[:, cols], preferred_element_type=F32)
        u = jnp.dot(xn, wu_ref[:, cols], preferred_element_type=F32)
        a = (g * jax.nn.sigmoid(g) * u).astype(BF16)
        d = jnp.dot(a, wd_ref[cols, :], preferred_element_type=F32)
        acc = d if acc is None else acc + d
    return acc


def _ffn_kernel(x_ref, g_ref, wg_ref, wu_ref, wd_ref, o_ref):
    x = x_ref[...]
    xn = _rms_norm(x, g_ref[...]).astype(BF16)
    o_ref[...] = x + 0.5 * _swiglu_tile(xn, wg_ref, wu_ref, wd_ref)


def _ffn(x, g, wg, wu, wd, tm):
    t, d = x.shape
    tile = pl.BlockSpec((tm, d), lambda i: (i, 0))
    return pl.pallas_call(
        _ffn_kernel,
        grid=(t // tm,),
        in_specs=[tile, _const_spec(g.shape), _const_spec(wg.shape), _const_spec(wu.shape), _const_spec(wd.shape)],
        out_specs=tile,
        out_shape=jax.ShapeDtypeStruct((t, d), F32),
        compiler_params=_params(1, 48),
        name="ffn",
    )(x, g, wg, wu, wd)


def _mix_in_kernel(h_ref, g_ref, pos_ref, invf_ref, sgn_ref, wr_ref, wfk_ref, wqvt_ref, wff_ref, bf_ref,
                   place_ref, wm_ref, bm_ref, r_ref, fk_ref, fqvt_ref, gate_ref, carry_ref, *,
                   ret_scale, fox_scale):
    tm = h_ref.shape[1]
    u = _rms_norm(h_ref[0], g_ref[...]).astype(BF16)

    pr = jnp.dot(u, wr_ref[...], preferred_element_type=F32)
    first = lax.broadcasted_iota(jnp.int32, (tm // 2, LANES), 1) < RET_HEAD_DIM // 2
    pos = pos_ref[0]
    ang = jnp.where(first, pos[:tm // 2], pos[tm // 2:]).astype(F32) * invf_ref[...]
    cos_h, sin_h = jnp.cos(ang), jnp.sin(ang)
    cos_r, sin_r = pltpu.roll(cos_h, RET_HEAD_DIM // 2, axis=1), pltpu.roll(sin_h, RET_HEAD_DIM // 2, axis=1)
    cos = jnp.concatenate([jnp.where(first, cos_h, cos_r), jnp.where(first, cos_r, cos_h)], axis=0)
    sin = jnp.concatenate([jnp.where(first, sin_h, sin_r), jnp.where(first, sin_r, sin_h)], axis=0)
    sin = sin * sgn_ref[...]
    for part, scale in ((0, ret_scale), (1, 1.0)):
        for hd in range(RET_HEADS):
            lo = part * RET_WIDTH + hd * RET_HEAD_DIM
            xh = pr[:, lo:lo + RET_HEAD_DIM]
            rot = xh * cos + pltpu.roll(xh, RET_HEAD_DIM // 2, axis=1) * sin
            if scale != 1.0:
                rot = rot * scale
            r_ref[0, :, lo:lo + RET_HEAD_DIM] = rot.astype(BF16)
    r_ref[0, :, 2 * RET_WIDTH:3 * RET_WIDTH] = pr[:, 2 * RET_WIDTH:3 * RET_WIDTH].astype(BF16)
    rg = pr[:, 3 * RET_WIDTH:]
    r_ref[0, :, 3 * RET_WIDTH:] = (rg * jax.nn.sigmoid(rg)).astype(BF16)

    fk = jnp.dot(u, wfk_ref[...], preferred_element_type=F32).astype(BF16)
    qvt = lax.dot_general(wqvt_ref[...], u, _NT, preferred_element_type=F32)
    fqvt_ref[0, :FOX_WIDTH, :] = (qvt[:FOX_WIDTH] * fox_scale).astype(BF16)
    fqvt_ref[0, FOX_WIDTH:, :] = qvt[FOX_WIDTH:].astype(BF16)

    ff = jnp.dot(u, wff_ref[...], preferred_element_type=F32) + bf_ref[...]
    logf = jnp.minimum(ff, 0.0) - jnp.log1p(jnp.exp(-jnp.abs(ff)))
    tok = lax.broadcasted_iota(jnp.int32, logf.shape, 0)
    cum = logf
    sh = 1
    while sh < tm:
        cum = cum + jnp.where(tok >= sh, pltpu.roll(cum, sh, axis=0), 0.0)
        sh *= 2

    @pl.when(pl.program_id(1) == 0)
    def _():
        carry_ref[...] = jnp.zeros_like(carry_ref)

    cum = cum + carry_ref[0:1, :]
    carry_ref[...] = jnp.broadcast_to(cum[tm - 1:tm, :], carry_ref.shape)

    c2 = cum * LOG2E
    c_hi = c2.astype(BF16)
    rest = c2 - c_hi.astype(F32)
    c_mid = rest.astype(BF16)
    c_lo = (rest - c_mid.astype(F32)).astype(BF16)
    placed = jnp.dot(jnp.concatenate([c_hi, c_mid, c_lo], axis=1), place_ref[...],
                     preferred_element_type=F32).astype(BF16)
    for pr_i in range(FOX_WIDTH // LANES):
        fk_ref[0, :, 2 * pr_i * LANES:(2 * pr_i + 1) * LANES] = fk[:, pr_i * LANES:(pr_i + 1) * LANES]
        fk_ref[0, :, (2 * pr_i + 1) * LANES:(2 * pr_i + 2) * LANES] = placed[:, pr_i * LANES:(pr_i + 1) * LANES]

    gm = jnp.dot(u, wm_ref[...], preferred_element_type=F32) + bm_ref[...]
    gate_ref[0] = jax.nn.sigmoid(gm).astype(BF16)


def _forget_placement():
    place = np.zeros((3 * LANES, FOX_WIDTH), np.float32)
    for head in range(FOX_HEADS):
        for piece in range(3):
            place[piece * LANES + head, (head // 2) * LANES + 3 * (head % 2) + piece] = 1.0
    return jnp.asarray(place, dtype=BF16)


def _mix_in(h, g, pos, invf, sgn, wr, wfk, wqvt, wff, bfg, wm, bm, tm):
    b, s, d = h.shape
    kern = functools.partial(_mix_in_kernel, ret_scale=RET_HEAD_DIM ** -0.5,
                             fox_scale=FOX_HEAD_DIM ** -0.5 * LOG2E)
    place = _forget_placement()
    tile = lambda w: pl.BlockSpec((1, tm, w), lambda i, j: (i, j, 0))
    return pl.pallas_call(
        kern,
        grid=(b, s // tm),
        in_specs=[tile(d), _const_spec(g.shape), tile(1), _const_spec(invf.shape), _const_spec(sgn.shape),
                  _const_spec(wr.shape),
                  _const_spec(wfk.shape), _const_spec(wqvt.shape), _const_spec(wff.shape), _const_spec(bfg.shape),
                  _const_spec(place.shape), _const_spec(wm.shape), _const_spec(bm.shape)],
        out_specs=[tile(4 * RET_WIDTH), tile(2 * FOX_WIDTH),
                   pl.BlockSpec((1, 2 * FOX_WIDTH, tm), lambda i, j: (i, 0, j)), tile(2 * d)],
        out_shape=[jax.ShapeDtypeStruct((b, s, 4 * RET_WIDTH), BF16),
                   jax.ShapeDtypeStruct((b, s, 2 * FOX_WIDTH), BF16),
                   jax.ShapeDtypeStruct((b, 2 * FOX_WIDTH, s), BF16),
                   jax.ShapeDtypeStruct((b, s, 2 * d), BF16)],
        scratch_shapes=[pltpu.VMEM((8, LANES), F32)],
        compiler_params=_params(2, 48),
        name="mix_in",
    )(h, g, pos, invf, sgn, wr, wfk, wqvt, wff, bfg, place, wm, bm)


def _retention_kernel(q_ref, k_ref, v_ref, g_ref, dmask_ref, qdec_ref, kdec_ref, o_ref, state_ref, *, chunk_decay):
    tr = q_ref.shape[1]

    @pl.when(pl.program_id(1) == 0)
    def _():
        state_ref[...] = jnp.zeros_like(state_ref)

    def tile(ref, c, hd):
        return ref[0, c * RET_CHUNK:(c + 1) * RET_CHUNK, hd * RET_HEAD_DIM:(hd + 1) * RET_HEAD_DIM]

    blocks = [(c, hd) for c in range(tr // RET_CHUNK) for hd in range(RET_HEADS)]
    qk, kv = {}, {}
    for c, hd in blocks:
        k = tile(k_ref, c, hd)
        qk[c, hd] = lax.dot_general(tile(q_ref, c, hd), k, _NT, preferred_element_type=F32)
        k_dec = (k.astype(F32) * kdec_ref[hd]).astype(BF16)
        kv[c, hd] = lax.dot_general(k_dec, tile(v_ref, c, hd), _TN, preferred_element_type=F32)
    states = [state_ref[hd] for hd in range(RET_HEADS)]
    for c, hd in blocks:
        q = tile(q_ref, c, hd)
        scores = (qk[c, hd] * dmask_ref[hd]).astype(BF16)
        q_dec = (q.astype(F32) * qdec_ref[hd]).astype(BF16)
        y = jnp.dot(jnp.concatenate([scores, q_dec], axis=1),
                    jnp.concatenate([tile(v_ref, c, hd), states[hd].astype(BF16)], axis=0),
                    preferred_element_type=F32)
        states[hd] = chunk_decay[hd] * states[hd] + kv[c, hd]
        mu = jnp.mean(y, axis=-1, keepdims=True)
        yc = y - mu
        var = jnp.mean(yc * yc, axis=-1, keepdims=True)
        yn = yc * lax.rsqrt(var + EPS)
        rows = slice(c * RET_CHUNK, (c + 1) * RET_CHUNK)
        cols = slice(hd * RET_HEAD_DIM, (hd + 1) * RET_HEAD_DIM)
        o_ref[0, rows, cols] = (yn * g_ref[0, rows, cols].astype(F32)).astype(BF16)
    for hd in range(RET_HEADS):
        state_ref[hd] = states[hd]


def _retention(r, dmask, qdec, kdec, chunk_decay, tr):
    b, s, _ = r.shape
    kern = functools.partial(_retention_kernel, chunk_decay=chunk_decay)
    part = lambda p: pl.BlockSpec((1, tr, RET_WIDTH), lambda i, j: (i, j, p))
    return pl.pallas_call(
        kern,
        grid=(b, s // tr),
        in_specs=[part(0), part(1), part(2), part(3),
                  _const_spec(dmask.shape), _const_spec(qdec.shape), _const_spec(kdec.shape)],
        out_specs=pl.BlockSpec((1, tr, RET_WIDTH), lambda i, j: (i, j, 0)),
        out_shape=jax.ShapeDtypeStruct((b, s, RET_WIDTH), BF16),
        scratch_shapes=[pltpu.VMEM((RET_HEADS, RET_HEAD_DIM, RET_HEAD_DIM), F32)],
        compiler_params=_params(2, 32),
        name="retention",
    )(r, r, r, r, dmask, qdec, kdec)


def _fox_schedule(nq):
    remaining = {qi: list(range(qi)) for qi in range(1, nq)}
    rounds = []
    while any(remaining.values()):
        pick = sorted((q for q in remaining if remaining[q]), key=lambda q: (-len(remaining[q]), q))[:FOX_ROUND]
        rounds.append([(q, remaining[q].pop(), q) for q in pick])
    n_plain = len(rounds)
    diag = [(q, q, q) for q in range(nq)]
    rounds += [diag[i:i + FOX_ROUND] for i in range(0, nq, FOX_ROUND)]
    flat = [t for rnd in rounds for t in rnd + [(0, 0, nq)] * (FOX_ROUND - len(rnd))]
    tables = tuple(jnp.asarray([t[i] for t in flat], dtype=jnp.int32) for i in range(3))
    return tables, n_plain, len(rounds)


def _fox_kernel(qsrc_ref, ksrc_ref, slot_ref, qt_ref, k_ref, vt_ref, o_ref, m_ref, acc_ref, *,
                tq, n_plain, n_rounds):
    s_len = k_ref.shape[1]
    nq = s_len // tq

    m_ref[...] = jnp.full(m_ref.shape, NEG_BIG, F32)
    acc_ref[...] = jnp.zeros(acc_ref.shape, F32)

    key_idx = lax.broadcasted_iota(jnp.int32, (tq, tq), 0)
    qry_idx = lax.broadcasted_iota(jnp.int32, (tq, tq), 1)
    causal = key_idx <= qry_idx
    half_zero = jnp.zeros((FOX_HEAD_DIM, tq), BF16)
    ones_rows = jnp.ones((FOX_ACC_ROWS - FOX_HEAD_DIM, tq), BF16)
    piece_row = lax.broadcasted_iota(jnp.int32, (LANES, tq), 0)
    pick_c = [jnp.where((piece_row >= 3 * hd) & (piece_row < 3 * hd + 3), -1.0, 0.0).astype(BF16) for hd in range(2)]
    chains = [(t, hd) for t in range(FOX_ROUND) for hd in range(2)]

    def run_round(rnd, masked):
        slots = [slot_ref[rnd * FOX_ROUND + t] for t in range(FOX_ROUND)]
        k0s = [pl.multiple_of(ksrc_ref[rnd * FOX_ROUND + t] * tq, tq) for t in range(FOX_ROUND)]
        qk = {}

        def scores(t):
            q0 = pl.multiple_of(qsrc_ref[rnd * FOX_ROUND + t] * tq, tq)
            qt = qt_ref[0, :, pl.ds(q0, tq)]
            qts = (jnp.concatenate([qt[:FOX_HEAD_DIM], half_zero, pick_c[0]], axis=0),
                   jnp.concatenate([half_zero, qt[FOX_HEAD_DIM:], pick_c[1]], axis=0))
            k2 = k_ref[0, pl.ds(k0s[t], tq), :]
            for hd in range(2):
                qk[t, hd] = jnp.dot(k2, qts[hd], preferred_element_type=F32)

        for t in range(FOX_LOOKAHEAD):
            scores(t)
        for t, hd in chains:
            if hd == 0 and t + FOX_LOOKAHEAD < FOX_ROUND:
                scores(t + FOX_LOOKAHEAD)
            s = qk[t, hd]
            if masked:
                s = jnp.where(causal, s, NEG_BIG)
            m_old = m_ref[slots[t], hd]
            m_new = jnp.maximum(m_old, jnp.max(s, axis=0, keepdims=True))
            m_ref[slots[t], hd] = m_new
            p = jnp.exp2(s - m_new).astype(BF16)
            vt = vt_ref[0, hd * FOX_HEAD_DIM:(hd + 1) * FOX_HEAD_DIM, pl.ds(k0s[t], tq)]
            vt = jnp.concatenate([vt, ones_rows], axis=0)
            acc_ref[slots[t], hd] = (jnp.exp2(m_old - m_new) * acc_ref[slots[t], hd]
                                     + jnp.dot(vt, p, preferred_element_type=F32))
        return 0

    lax.fori_loop(0, n_plain, lambda rnd, _: run_round(rnd, False), 0)
    lax.fori_loop(n_plain, n_rounds, lambda rnd, _: run_round(rnd, True), 0)

    def finish(qi, _):
        outs = []
        for hd in range(2):
            acc = acc_ref[qi, hd]
            outs.append(acc[:FOX_HEAD_DIM] / acc[FOX_HEAD_DIM:FOX_HEAD_DIM + 1])
        q0 = pl.multiple_of(qi * tq, tq)
        o_ref[0, pl.ds(q0, tq), :] = jnp.concatenate(outs, axis=0).T.astype(BF16)
        return 0

    lax.fori_loop(0, nq, finish, 0)


def _fox(fk, fqvt, tq):
    b, s, _ = fk.shape
    nq = s // tq
    pairs = FOX_WIDTH // LANES
    tables, n_plain, n_rounds = _fox_schedule(nq)
    kern = functools.partial(_fox_kernel, tq=tq, n_plain=n_plain, n_rounds=n_rounds)
    grid_spec = pltpu.PrefetchScalarGridSpec(
        num_scalar_prefetch=len(tables),
        grid=(b, pairs),
        in_specs=[pl.BlockSpec((1, LANES, s), lambda i, j, *_: (i, j, 0)),
                  pl.BlockSpec((1, s, 2 * LANES), lambda i, j, *_: (i, 0, j)),
                  pl.BlockSpec((1, LANES, s), lambda i, j, *_: (i, pairs + j, 0))],
        out_specs=pl.BlockSpec((1, s, LANES), lambda i, j, *_: (i, 0, j)),
        scratch_shapes=[pltpu.VMEM((nq + 1, 2, 1, tq), F32),
                        pltpu.VMEM((nq + 1, 2, FOX_ACC_ROWS, tq), F32)],
    )
    return pl.pallas_call(
        kern,
        grid_spec=grid_spec,
        out_shape=jax.ShapeDtypeStruct((b, s, FOX_WIDTH), BF16),
        compiler_params=_params(2, 40),
        name="fox",
    )(*tables, fqvt, fk, fqvt)


def _post_kernel(h_ref, yr_ref, yf_ref, gate_ref, p_ref, wro_ref, wfo_ref, wo_ref, g2_ref, wg_ref, wu_ref, wd_ref,
                 gp_ref, wpg_ref, wp_ref, gf_ref, o_ref):
    d = h_ref.shape[1]
    z_a = jnp.dot(yr_ref[...], wro_ref[...], preferred_element_type=F32)
    z_b = jnp.dot(yf_ref[...], wfo_ref[...], preferred_element_type=F32)
    mixed = gate_ref[:, :d].astype(F32) * z_a + gate_ref[:, d:].astype(F32) * z_b
    h = h_ref[...] + jnp.dot(mixed.astype(BF16), wo_ref[...], preferred_element_type=F32)
    h = h + 0.5 * _swiglu_tile(_rms_norm(h, g2_ref[...]).astype(BF16), wg_ref, wu_ref, wd_ref)
    hn = _rms_norm(h, gp_ref[...]).astype(BF16)
    ple_gate = jax.nn.sigmoid(jnp.dot(hn, wpg_ref[...], preferred_element_type=F32))
    emb = jnp.dot(p_ref[...].astype(BF16), wp_ref[...], preferred_element_type=F32)
    o_ref[...] = _rms_norm(h + ple_gate * emb, gf_ref[...])


def _post(h, yr, yf, gates, p, wro, wfo, wo, g2, wg, wu, wd, gp, wpg, wp, gf, tm):
    t, d = h.shape
    tile = lambda w: pl.BlockSpec((tm, w), lambda i: (i, 0))
    consts = [wro, wfo, wo, g2, wg, wu, wd, gp, wpg, wp, gf]
    return pl.pallas_call(
        _post_kernel,
        grid=(t // tm,),
        in_specs=[tile(d), tile(RET_WIDTH), tile(FOX_WIDTH), tile(2 * d), tile(p.shape[1])]
        + [_const_spec(c.shape) for c in consts],
        out_specs=tile(d),
        out_shape=jax.ShapeDtypeStruct((t, d), F32),
        compiler_params=_params(1, 58),
        name="post",
    )(h, yr, yf, gates, p, *consts)


def _retention_tables():
    hd = np.arange(RET_HEADS, dtype=np.float64)
    log_gamma = np.log1p(-np.exp2(-5.0 - hd))
    idx = np.arange(RET_CHUNK, dtype=np.float64)
    diff = idx[:, None] - idx[None, :]
    dmask = np.where(diff >= 0, np.exp(log_gamma[:, None, None] * np.maximum(diff, 0.0)), 0.0)
    ones = np.ones((1, 1, RET_HEAD_DIM))
    qdec = np.exp(log_gamma[:, None] * (idx + 1.0))[:, :, None] * ones
    kdec = np.exp(log_gamma[:, None] * (RET_CHUNK - 1 - idx))[:, :, None] * ones
    chunk_decay = tuple(float(x) for x in np.exp(log_gamma * RET_CHUNK))
    as_f32 = lambda a: jnp.asarray(a, dtype=F32)
    return as_f32(dmask), as_f32(qdec), as_f32(kdec), chunk_decay


def _layer(h, p, positions, ln_ffn1, w_ffn1_gate, w_ffn1_up, w_ffn1_down, ln_mix, w_in, b_forget, w_merge,
           b_merge, w_ret_out, w_fox_out, w_out, ln_ffn2, w_ffn2_gate, w_ffn2_up, w_ffn2_down, ln_ple, w_ple,
           w_ple_gate, ln_final):
    b, s, d = h.shape
    t = b * s
    tm = min(512, s)
    row = lambda v: v.reshape(1, -1).astype(F32)

    inv_freq = 1.0 / (ROPE_BASE ** (jnp.arange(RET_HEAD_DIM // 2, dtype=F32) / (RET_HEAD_DIM // 2)))
    invf = jnp.concatenate([inv_freq, inv_freq]).reshape(1, RET_HEAD_DIM)
    sgn = jnp.concatenate([-jnp.ones((RET_HEAD_DIM // 2,), F32), jnp.ones((RET_HEAD_DIM // 2,), F32)]).reshape(1, -1)
    h1 = _ffn(h.reshape(t, d), row(ln_ffn1), w_ffn1_gate.astype(BF16), w_ffn1_up.astype(BF16),
              w_ffn1_down.astype(BF16), tm)

    even_odd = lambda w: w.reshape(d, RET_HEADS, RET_HEAD_DIM // 2, 2).transpose(0, 1, 3, 2).reshape(d, RET_WIDTH)
    w_rq = even_odd(w_in[:, :RET_WIDTH])
    w_rk = even_odd(w_in[:, RET_WIDTH:2 * RET_WIDTH])
    fox_lo = 4 * RET_WIDTH
    fox_hi = fox_lo + 3 * FOX_WIDTH
    wr = jnp.concatenate([w_rq, w_rk, w_in[:, 2 * RET_WIDTH:fox_lo]], axis=1).astype(BF16)
    w_fq, w_fk, w_fv = (w_in[:, fox_lo + i * FOX_WIDTH:fox_lo + (i + 1) * FOX_WIDTH] for i in range(3))
    wqvt = jnp.concatenate([w_fq, w_fv], axis=1).T.astype(BF16)
    wff = jnp.pad(w_in[:, fox_hi:], ((0, 0), (0, LANES - FOX_HEADS))).astype(BF16)
    bfg = jnp.pad(b_forget.astype(F32), (0, LANES - FOX_HEADS)).reshape(1, LANES)
    r, fk, fqvt, gates = _mix_in(h1.reshape(b, s, d), row(ln_mix), positions.reshape(b, s, 1), invf, sgn, wr,
                                 w_fk.astype(BF16), wqvt, wff, bfg, w_merge.astype(BF16), row(b_merge), tm)

    dmask, qdec, kdec, chunk_decay = _retention_tables()
    y_ret = _retention(r, dmask, qdec, kdec, chunk_decay, min(512, s))
    y_fox = _fox(fk, fqvt, min(256, s))

    assert ln_final is not None, "multi-layer stacks need a post kernel without the final norm"
    out = _post(h1, y_ret.reshape(t, RET_WIDTH), y_fox.reshape(t, FOX_WIDTH), gates.reshape(t, 2 * d),
                p.reshape(t, -1), w_ret_out.astype(BF16), w_fox_out.astype(BF16), w_out.astype(BF16), row(ln_ffn2),
                w_ffn2_gate.astype(BF16), w_ffn2_up.astype(BF16), w_ffn2_down.astype(BF16), row(ln_ple),
                w_ple_gate.astype(BF16), w_ple.astype(BF16), row(ln_final), tm)
    return out.reshape(b, s, d)


def kernel(x, p, positions, ln_ffn1, w_ffn1_gate, w_ffn1_up, w_ffn1_down, ln_mix, w_in, b_forget, w_merge, b_merge,
           w_ret_out, w_fox_out, w_out, ln_ffn2, w_ffn2_gate, w_ffn2_up, w_ffn2_down, ln_ple, w_ple, w_ple_gate,
           ln_final):
    depth = p.shape[0]
    assert depth == 1, "this implementation covers the single-layer configuration"
    return _layer(x, p[0], positions, ln_ffn1[0], w_ffn1_gate[0], w_ffn1_up[0], w_ffn1_down[0], ln_mix[0], w_in[0],
                  b_forget[0], w_merge[0], b_merge[0], w_ret_out[0], w_fox_out[0], w_out[0], ln_ffn2[0],
                  w_ffn2_gate[0], w_ffn2_up[0], w_ffn2_down[0], ln_ple[0], w_ple[0], w_ple_gate[0], ln_final)
```

```python
import functools

import numpy as np
import jax
import jax.numpy as jnp
from jax import lax
from jax.experimental import pallas as pl
from jax.experimental.pallas import tpu as pltpu

F32 = jnp.float32
BF16 = jnp.bfloat16

EPS = 1e-6
ROPE_BASE = 10000.0
RET_HEADS = 4
RET_HEAD_DIM = 128
RET_WIDTH = RET_HEADS * RET_HEAD_DIM
FOX_HEADS = 8
FOX_HEAD_DIM = 64
FOX_WIDTH = FOX_HEADS * FOX_HEAD_DIM
RET_CHUNK = 128

LANES = 128
FF_CHUNK = 256
NEG_BIG = -1e30
LOG2E = 1.4426950408889634
FOX_ROUND = 8
FOX_LOOKAHEAD = 2
FOX_ACC_ROWS = FOX_HEAD_DIM + 16

_NT = (((1,), (1,)), ((), ()))
_TN = (((0,), (0,)), ((), ()))


def _const_spec(shape):
    zeros = (0,) * len(shape)
    return pl.BlockSpec(shape, lambda *_: zeros, pipeline_mode=pl.Buffered(1))


def _params(n_axes, vmem_mb):
    return pltpu.CompilerParams(
        dimension_semantics=("arbitrary",) * n_axes,
        vmem_limit_bytes=vmem_mb * 1024 * 1024,
    )


def _rms_norm(x, g):
    return x * lax.rsqrt(jnp.mean(x * x, axis=-1, keepdims=True) + EPS) * g


def _swiglu_tile(xn, wg_ref, wu_ref, wd_ref):
    acc = None
    for c in range(wg_ref.shape[1] // FF_CHUNK):
        cols = slice(c * FF_CHUNK, (c + 1) * FF_CHUNK)
        g = jnp.dot(xn, wg_ref[:, cols], preferred_element_type=F32)
        u = jnp.dot(xn, wu_ref[:, cols], preferred_element_type=F32)
        a = (g * jax.nn.sigmoid(g) * u).astype(BF16)
        d = jnp.dot(a, wd_ref[cols, :], preferred_element_type=F32)
        acc = d if acc is None else acc + d
    return acc


def _ffn_kernel(x_ref, g_ref, wg_ref, wu_ref, wd_ref, o_ref):
    x = x_ref[...]
    xn = _rms_norm(x, g_ref[...]).astype(BF16)
    o_ref[...] = x + 0.5 * _swiglu_tile(xn, wg_ref, wu_ref, wd_ref)


def _ffn(x, g, wg, wu, wd, tm):
    t, d = x.shape
    tile = pl.BlockSpec((tm, d), lambda i: (i, 0))
    return pl.pallas_call(
        _ffn_kernel,
        grid=(t // tm,),
        in_specs=[tile, _const_spec(g.shape), _const_spec(wg.shape), _const_spec(wu.shape), _const_spec(wd.shape)],
        out_specs=tile,
        out_shape=jax.ShapeDtypeStruct((t, d), F32),
        compiler_params=_params(1, 48),
        name="ffn",
    )(x, g, wg, wu, wd)


def _mix_in_kernel(h_ref, g_ref, pos_ref, invf_ref, sgn_ref, wr_ref, wfk_ref, wqvt_ref, bf_ref,
                   place_ref, wm_ref, bm_ref, r_ref, fk_ref, fqvt_ref, gate_ref, carry_ref, *,
                   ret_scale, fox_scale):
    tm = h_ref.shape[1]
    u = _rms_norm(h_ref[0], g_ref[...]).astype(BF16)

    gm = jnp.dot(u, wm_ref[...], preferred_element_type=F32) + bm_ref[...]
    gate_ref[0] = jax.nn.sigmoid(gm).astype(BF16)

    pr = jnp.dot(u, wr_ref[...], preferred_element_type=F32)
    qk0 = LANES
    ff = pr[:, :LANES] + bf_ref[...]
    logf = jnp.minimum(ff, 0.0) - jnp.log1p(jnp.exp(-jnp.abs(ff)))
    tok = lax.broadcasted_iota(jnp.int32, logf.shape, 0)
    cum = logf
    sh = 1
    while sh < tm:
        cum = cum + jnp.where(tok >= sh, pltpu.roll(cum, sh, axis=0), 0.0)
        sh *= 2

    @pl.when(pl.program_id(1) == 0)
    def _():
        carry_ref[...] = jnp.zeros_like(carry_ref)

    cum = cum + carry_ref[0:1, :]
    carry_ref[...] = jnp.broadcast_to(cum[tm - 1:tm, :], carry_ref.shape)

    c2 = cum * LOG2E
    c_hi = c2.astype(BF16).astype(F32)
    rest = c2 - c_hi
    c_mid = rest.astype(BF16).astype(F32)
    c_lo = rest - c_mid
    lane = lax.broadcasted_iota(jnp.int32, c2.shape, 1)
    pieces = jnp.where(lane < FOX_HEADS, c_hi,
                       jnp.where(lane < 2 * FOX_HEADS, pltpu.roll(c_mid, FOX_HEADS, axis=1),
                                 pltpu.roll(c_lo, 2 * FOX_HEADS, axis=1))).astype(BF16)

    first = lax.broadcasted_iota(jnp.int32, (tm // 2, LANES), 1) < RET_HEAD_DIM // 2
    pos = pos_ref[0]
    ang = jnp.where(first, pos[:tm // 2], pos[tm // 2:]).astype(F32) * invf_ref[...]
    cos_h, sin_h = jnp.cos(ang), jnp.sin(ang)
    cos_r, sin_r = pltpu.roll(cos_h, RET_HEAD_DIM // 2, axis=1), pltpu.roll(sin_h, RET_HEAD_DIM // 2, axis=1)
    cos = jnp.concatenate([jnp.where(first, cos_h, cos_r), jnp.where(first, cos_r, cos_h)], axis=0)
    sin = jnp.concatenate([jnp.where(first, sin_h, sin_r), jnp.where(first, sin_r, sin_h)], axis=0)
    sin = sin * sgn_ref[...]
    for part, scale in ((0, ret_scale), (1, 1.0)):
        for hd in range(RET_HEADS):
            lo = part * RET_WIDTH + hd * RET_HEAD_DIM
            xh = pr[:, qk0 + lo:qk0 + lo + RET_HEAD_DIM]
            rot = xh * cos + pltpu.roll(xh, RET_HEAD_DIM // 2, axis=1) * sin
            if scale != 1.0:
                rot = rot * scale
            r_ref[0, :, lo:lo + RET_HEAD_DIM] = rot.astype(BF16)
    r_ref[0, :, 2 * RET_WIDTH:3 * RET_WIDTH] = pr[:, qk0 + 2 * RET_WIDTH:qk0 + 3 * RET_WIDTH].astype(BF16)
    rg = pr[:, qk0 + 3 * RET_WIDTH:]
    r_ref[0, :, 3 * RET_WIDTH:] = (rg * jax.nn.sigmoid(rg)).astype(BF16)

    fk = jnp.dot(u, wfk_ref[...], preferred_element_type=F32).astype(BF16)
    qvt = lax.dot_general(wqvt_ref[...], u, _NT, preferred_element_type=F32)
    fqvt_ref[0, :FOX_WIDTH, :] = (qvt[:FOX_WIDTH] * fox_scale).astype(BF16)
    fqvt_ref[0, FOX_WIDTH:, :] = qvt[FOX_WIDTH:].astype(BF16)

    placed = jnp.dot(pieces, place_ref[...], preferred_element_type=F32).astype(BF16)
    for pr_i in range(FOX_WIDTH // LANES):
        fk_ref[0, :, 2 * pr_i * LANES:(2 * pr_i + 1) * LANES] = fk[:, pr_i * LANES:(pr_i + 1) * LANES]
        fk_ref[0, :, (2 * pr_i + 1) * LANES:(2 * pr_i + 2) * LANES] = placed[:, pr_i * LANES:(pr_i + 1) * LANES]


def _forget_placement():
    place = np.zeros((LANES, FOX_WIDTH), np.float32)
    for head in range(FOX_HEADS):
        for piece in range(3):
            place[piece * FOX_HEADS + head, (head // 2) * LANES + 3 * (head % 2) + piece] = 1.0
    return jnp.asarray(place, dtype=BF16)


def _mix_in(h, g, pos, invf, sgn, wr, wfk, wqvt, bfg, wm, bm, tm):
    b, s, d = h.shape
    kern = functools.partial(_mix_in_kernel, ret_scale=RET_HEAD_DIM ** -0.5,
                             fox_scale=FOX_HEAD_DIM ** -0.5 * LOG2E)
    place = _forget_placement()
    tile = lambda w: pl.BlockSpec((1, tm, w), lambda i, j: (i, j, 0))
    return pl.pallas_call(
        kern,
        grid=(b, s // tm),
        in_specs=[tile(d), _const_spec(g.shape), tile(1), _const_spec(invf.shape), _const_spec(sgn.shape),
                  _const_spec(wr.shape),
                  _const_spec(wfk.shape), _const_spec(wqvt.shape), _const_spec(bfg.shape),
                  _const_spec(place.shape), _const_spec(wm.shape), _const_spec(bm.shape)],
        out_specs=[tile(4 * RET_WIDTH), tile(2 * FOX_WIDTH),
                   pl.BlockSpec((1, 2 * FOX_WIDTH, tm), lambda i, j: (i, 0, j)), tile(2 * d)],
        out_shape=[jax.ShapeDtypeStruct((b, s, 4 * RET_WIDTH), BF16),
                   jax.ShapeDtypeStruct((b, s, 2 * FOX_WIDTH), BF16),
                   jax.ShapeDtypeStruct((b, 2 * FOX_WIDTH, s), BF16),
                   jax.ShapeDtypeStruct((b, s, 2 * d), BF16)],
        scratch_shapes=[pltpu.VMEM((8, LANES), F32)],
        compiler_params=_params(2, 48),
        name="mix_in",
    )(h, g, pos, invf, sgn, wr, wfk, wqvt, bfg, place, wm, bm)


def _retention_kernel(q_ref, k_ref, v_ref, g_ref, dmask_ref, qdec_ref, kdec_ref, o_ref, state_ref, *, chunk_decay):
    tr = q_ref.shape[1]

    @pl.when(pl.program_id(1) == 0)
    def _():
        state_ref[...] = jnp.zeros_like(state_ref)

    def tile(ref, c, hd):
        return ref[0, c * RET_CHUNK:(c + 1) * RET_CHUNK, hd * RET_HEAD_DIM:(hd + 1) * RET_HEAD_DIM]

    blocks = [(c, hd) for c in range(tr // RET_CHUNK) for hd in range(RET_HEADS)]
    qk, kv = {}, {}
    for c, hd in blocks:
        k = tile(k_ref, c, hd)
        qk[c, hd] = lax.dot_general(tile(q_ref, c, hd), k, _NT, preferred_element_type=F32)
        k_dec = (k.astype(F32) * kdec_ref[hd]).astype(BF16)
        kv[c, hd] = lax.dot_general(k_dec, tile(v_ref, c, hd), _TN, preferred_element_type=F32)
    states = [state_ref[hd] for hd in range(RET_HEADS)]
    for c, hd in blocks:
        q = tile(q_ref, c, hd)
        scores = (qk[c, hd] * dmask_ref[hd]).astype(BF16)
        q_dec = (q.astype(F32) * qdec_ref[hd]).astype(BF16)
        y = jnp.dot(jnp.concatenate([scores, q_dec], axis=1),
                    jnp.concatenate([tile(v_ref, c, hd), states[hd].astype(BF16)], axis=0),
                    preferred_element_type=F32)
        states[hd] = chunk_decay[hd] * states[hd] + kv[c, hd]
        mu = jnp.mean(y, axis=-1, keepdims=True)
        yc = y - mu
        var = jnp.mean(yc * yc, axis=-1, keepdims=True)
        yn = yc * lax.rsqrt(var + EPS)
        rows = slice(c * RET_CHUNK, (c + 1) * RET_CHUNK)
        cols = slice(hd * RET_HEAD_DIM, (hd + 1) * RET_HEAD_DIM)
        o_ref[0, rows, cols] = (yn * g_ref[0, rows, cols].astype(F32)).astype(BF16)
    for hd in range(RET_HEADS):
        state_ref[hd] = states[hd]


def _retention(r, dmask, qdec, kdec, chunk_decay, tr):
    b, s, _ = r.shape
    kern = functools.partial(_retention_kernel, chunk_decay=chunk_decay)
    part = lambda p: pl.BlockSpec((1, tr, RET_WIDTH), lambda i, j: (i, j, p))
    return pl.pallas_call(
        kern,
        grid=(b, s // tr),
        in_specs=[part(0), part(1), part(2), part(3),
                  _const_spec(dmask.shape), _const_spec(qdec.shape), _const_spec(kdec.shape)],
        out_specs=pl.BlockSpec((1, tr, RET_WIDTH), lambda i, j: (i, j, 0)),
        out_shape=jax.ShapeDtypeStruct((b, s, RET_WIDTH), BF16),
        scratch_shapes=[pltpu.VMEM((RET_HEADS, RET_HEAD_DIM, RET_HEAD_DIM), F32)],
        compiler_params=_params(2, 32),
        name="retention",
    )(r, r, r, r, dmask, qdec, kdec)


def _fox_schedule(nq):
    remaining = {qi: list(range(qi)) for qi in range(1, nq)}
    rounds = []
    while any(remaining.values()):
        pick = sorted((q for q in remaining if remaining[q]), key=lambda q: (-len(remaining[q]), q))[:FOX_ROUND]
        rounds.append([(q, remaining[q].pop(), q) for q in pick])
    n_plain = len(rounds)
    diag = [(q, q, q) for q in range(nq)]
    rounds += [diag[i:i + FOX_ROUND] for i in range(0, nq, FOX_ROUND)]
    flat = [t for rnd in rounds for t in rnd + [(0, 0, nq)] * (FOX_ROUND - len(rnd))]
    tables = tuple(jnp.asarray([t[i] for t in flat], dtype=jnp.int32) for i in range(3))
    return tables, n_plain, len(rounds)


def _fox_kernel(qsrc_ref, ksrc_ref, slot_ref, qt_ref, k_ref, vt_ref, o_ref, m_ref, acc_ref, *,
                tq, n_plain, n_rounds):
    s_len = k_ref.shape[1]
    nq = s_len // tq

    m_ref[...] = jnp.full(m_ref.shape, NEG_BIG, F32)
    acc_ref[...] = jnp.zeros(acc_ref.shape, F32)

    key_idx = lax.broadcasted_iota(jnp.int32, (tq, tq), 0)
    qry_idx = lax.broadcasted_iota(jnp.int32, (tq, tq), 1)
    causal = key_idx <= qry_idx
    half_zero = jnp.zeros((FOX_HEAD_DIM, tq), BF16)
    ones_rows = jnp.ones((FOX_ACC_ROWS - FOX_HEAD_DIM, tq), BF16)
    piece_row = lax.broadcasted_iota(jnp.int32, (LANES, tq), 0)
    pick_c = [jnp.where((piece_row >= 3 * hd) & (piece_row < 3 * hd + 3), -1.0, 0.0).astype(BF16) for hd in range(2)]
    chains = [(t, hd) for t in range(FOX_ROUND) for hd in range(2)]

    def run_round(rnd, masked):
        slots = [slot_ref[rnd * FOX_ROUND + t] for t in range(FOX_ROUND)]
        k0s = [pl.multiple_of(ksrc_ref[rnd * FOX_ROUND + t] * tq, tq) for t in range(FOX_ROUND)]
        qk = {}

        def scores(t):
            q0 = pl.multiple_of(qsrc_ref[rnd * FOX_ROUND + t] * tq, tq)
            qt = qt_ref[0, :, pl.ds(q0, tq)]
            qts = (jnp.concatenate([qt[:FOX_HEAD_DIM], half_zero, pick_c[0]], axis=0),
                   jnp.concatenate([half_zero, qt[FOX_HEAD_DIM:], pick_c[1]], axis=0))
            k2 = k_ref[0, pl.ds(k0s[t], tq), :]
            for hd in range(2):
                qk[t, hd] = jnp.dot(k2, qts[hd], preferred_element_type=F32)

        for t in range(FOX_LOOKAHEAD):
            scores(t)
        for t, hd in chains:
            if hd == 0 and t + FOX_LOOKAHEAD < FOX_ROUND:
                scores(t + FOX_LOOKAHEAD)
            s = qk[t, hd]
            if masked:
                s = jnp.where(causal, s, NEG_BIG)
            m_old = m_ref[slots[t], hd]
            m_new = jnp.maximum(m_old, jnp.max(s, axis=0, keepdims=True))
            m_ref[slots[t], hd] = m_new
            p = jnp.exp2(s - m_new).astype(BF16)
            vt = vt_ref[0, hd * FOX_HEAD_DIM:(hd + 1) * FOX_HEAD_DIM, pl.ds(k0s[t], tq)]
            vt = jnp.concatenate([vt, ones_rows], axis=0)
            acc_ref[slots[t], hd] = (jnp.exp2(m_old - m_new) * acc_ref[slots[t], hd]
                                     + jnp.dot(vt, p, preferred_element_type=F32))
        return 0

    def sweep(lo, hi, masked):
        def two_rounds(i, _):
            run_round(lo + 2 * i, masked)
            run_round(lo + 2 * i + 1, masked)
            return 0
        lax.fori_loop(0, (hi - lo) // 2, two_rounds, 0)
        if (hi - lo) % 2:
            run_round(hi - 1, masked)

    sweep(0, n_plain, False)
    sweep(n_plain, n_rounds, True)

    def finish(qi, _):
        outs = []
        for hd in range(2):
            acc = acc_ref[qi, hd]
            outs.append(acc[:FOX_HEAD_DIM] / acc[FOX_HEAD_DIM:FOX_HEAD_DIM + 1])
        q0 = pl.multiple_of(qi * tq, tq)
        o_ref[0, :, pl.ds(q0, tq)] = jnp.concatenate(outs, axis=0).astype(BF16)
        return 0

    lax.fori_loop(0, nq, finish, 0)


def _fox(fk, fqvt, tq):
    b, s, _ = fk.shape
    nq = s // tq
    pairs = FOX_WIDTH // LANES
    tables, n_plain, n_rounds = _fox_schedule(nq)
    kern = functools.partial(_fox_kernel, tq=tq, n_plain=n_plain, n_rounds=n_rounds)
    grid_spec = pltpu.PrefetchScalarGridSpec(
        num_scalar_prefetch=len(tables),
        grid=(b, pairs),
        in_specs=[pl.BlockSpec((1, LANES, s), lambda i, j, *_: (i, j, 0)),
                  pl.BlockSpec((1, s, 2 * LANES), lambda i, j, *_: (i, 0, j)),
                  pl.BlockSpec((1, LANES, s), lambda i, j, *_: (i, pairs + j, 0))],
        out_specs=pl.BlockSpec((1, LANES, s), lambda i, j, *_: (i, j, 0)),
        scratch_shapes=[pltpu.VMEM((nq + 1, 2, 1, tq), F32),
                        pltpu.VMEM((nq + 1, 2, FOX_ACC_ROWS, tq), F32)],
    )
    return pl.pallas_call(
        kern,
        grid_spec=grid_spec,
        out_shape=jax.ShapeDtypeStruct((b, FOX_WIDTH, s), BF16),
        compiler_params=_params(2, 40),
        name="fox",
    )(*tables, fqvt, fk, fqvt)


def _post_kernel(h_ref, yr_ref, yf_ref, gate_ref, p_ref, wro_ref, wfo_ref, wo_ref, g2_ref, wg_ref, wu_ref, wd_ref,
                 gp_ref, wpg_ref, wp_ref, gf_ref, o_ref):
    d = h_ref.shape[1]
    z_a = jnp.dot(yr_ref[...], wro_ref[...], preferred_element_type=F32)
    z_b = lax.dot_general(yf_ref[0], wfo_ref[...], _TN, preferred_element_type=F32)
    mixed = gate_ref[:, :d].astype(F32) * z_a + gate_ref[:, d:].astype(F32) * z_b
    h = h_ref[...] + jnp.dot(mixed.astype(BF16), wo_ref[...], preferred_element_type=F32)
    h = h + 0.5 * _swiglu_tile(_rms_norm(h, g2_ref[...]).astype(BF16), wg_ref, wu_ref, wd_ref)
    hn = _rms_norm(h, gp_ref[...]).astype(BF16)
    ple_gate = jax.nn.sigmoid(jnp.dot(hn, wpg_ref[...], preferred_element_type=F32))
    emb = jnp.dot(p_ref[...].astype(BF16), wp_ref[...], preferred_element_type=F32)
    o_ref[...] = _rms_norm(h + ple_gate * emb, gf_ref[...])


def _post(h, yr, yf_t, gates, p, wro, wfo, wo, g2, wg, wu, wd, gp, wpg, wp, gf, tm):
    t, d = h.shape
    per_seq = yf_t.shape[2] // tm
    tile = lambda w: pl.BlockSpec((tm, w), lambda i: (i, 0))
    consts = [wro, wfo, wo, g2, wg, wu, wd, gp, wpg, wp, gf]
    return pl.pallas_call(
        _post_kernel,
        grid=(t // tm,),
        in_specs=[tile(d), tile(RET_WIDTH),
                  pl.BlockSpec((1, FOX_WIDTH, tm), lambda i: (i // per_seq, 0, i % per_seq)),
                  tile(2 * d), tile(p.shape[1])]
        + [_const_spec(c.shape) for c in consts],
        out_specs=tile(d),
        out_shape=jax.ShapeDtypeStruct((t, d), F32),
        compiler_params=_params(1, 58),
        name="post",
    )(h, yr, yf_t, gates, p, *consts)


def _retention_tables():
    hd = np.arange(RET_HEADS, dtype=np.float64)
    log_gamma = np.log1p(-np.exp2(-5.0 - hd))
    idx = np.arange(RET_CHUNK, dtype=np.float64)
    diff = idx[:, None] - idx[None, :]
    dmask = np.where(diff >= 0, np.exp(log_gamma[:, None, None] * np.maximum(diff, 0.0)), 0.0)
    ones = np.ones((1, 1, RET_HEAD_DIM))
    qdec = np.exp(log_gamma[:, None] * (idx + 1.0))[:, :, None] * ones
    kdec = np.exp(log_gamma[:, None] * (RET_CHUNK - 1 - idx))[:, :, None] * ones
    chunk_decay = tuple(float(x) for x in np.exp(log_gamma * RET_CHUNK))
    as_f32 = lambda a: jnp.asarray(a, dtype=F32)
    return as_f32(dmask), as_f32(qdec), as_f32(kdec), chunk_decay


def _layer(h, p, positions, ln_ffn1, w_ffn1_gate, w_ffn1_up, w_ffn1_down, ln_mix, w_in, b_forget, w_merge,
           b_merge, w_ret_out, w_fox_out, w_out, ln_ffn2, w_ffn2_gate, w_ffn2_up, w_ffn2_down, ln_ple, w_ple,
           w_ple_gate, ln_final):
    b, s, d = h.shape
    t = b * s
    tm = min(512, s)
    row = lambda v: v.reshape(1, -1).astype(F32)

    inv_freq = 1.0 / (ROPE_BASE ** (jnp.arange(RET_HEAD_DIM // 2, dtype=F32) / (RET_HEAD_DIM // 2)))
    invf = jnp.concatenate([inv_freq, inv_freq]).reshape(1, RET_HEAD_DIM)
    sgn = jnp.concatenate([-jnp.ones((RET_HEAD_DIM // 2,), F32), jnp.ones((RET_HEAD_DIM // 2,), F32)]).reshape(1, -1)
    h1 = _ffn(h.reshape(t, d), row(ln_ffn1), w_ffn1_gate.astype(BF16), w_ffn1_up.astype(BF16),
              w_ffn1_down.astype(BF16), tm)

    even_odd = lambda w: w.reshape(d, RET_HEADS, RET_HEAD_DIM // 2, 2).transpose(0, 1, 3, 2).reshape(d, RET_WIDTH)
    w_rq = even_odd(w_in[:, :RET_WIDTH])
    w_rk = even_odd(w_in[:, RET_WIDTH:2 * RET_WIDTH])
    fox_lo = 4 * RET_WIDTH
    fox_hi = fox_lo + 3 * FOX_WIDTH
    w_ff = jnp.pad(w_in[:, fox_hi:], ((0, 0), (0, LANES - FOX_HEADS)))
    wr = jnp.concatenate([w_ff, w_rq, w_rk, w_in[:, 2 * RET_WIDTH:fox_lo]], axis=1).astype(BF16)
    w_fq, w_fk, w_fv = (w_in[:, fox_lo + i * FOX_WIDTH:fox_lo + (i + 1) * FOX_WIDTH] for i in range(3))
    wqvt = jnp.concatenate([w_fq, w_fv], axis=1).T.astype(BF16)
    bfg = jnp.pad(b_forget.astype(F32), (0, LANES - FOX_HEADS)).reshape(1, LANES)
    r, fk, fqvt, gates = _mix_in(h1.reshape(b, s, d), row(ln_mix), positions.reshape(b, s, 1), invf, sgn, wr,
                                 w_fk.astype(BF16), wqvt, bfg, w_merge.astype(BF16), row(b_merge), tm)

    dmask, qdec, kdec, chunk_decay = _retention_tables()
    y_ret = _retention(r, dmask, qdec, kdec, chunk_decay, min(512, s))
    y_fox = _fox(fk, fqvt, min(256, s))

    assert ln_final is not None, "multi-layer stacks need a post kernel without the final norm"
    out = _post(h1, y_ret.reshape(t, RET_WIDTH), y_fox, gates.reshape(t, 2 * d),
                p.reshape(t, -1), w_ret_out.astype(BF16), w_fox_out.astype(BF16), w_out.astype(BF16), row(ln_ffn2),
                w_ffn2_gate.astype(BF16), w_ffn2_up.astype(BF16), w_ffn2_down.astype(BF16), row(ln_ple),
                w_ple_gate.astype(BF16), w_ple.astype(BF16), row(ln_final), tm)
    return out.reshape(b, s, d)


def kernel(x, p, positions, ln_ffn1, w_ffn1_gate, w_ffn1_up, w_ffn1_down, ln_mix, w_in, b_forget, w_merge, b_merge,
           w_ret_out, w_fox_out, w_out, ln_ffn2, w_ffn2_gate, w_ffn2_up, w_ffn2_down, ln_ple, w_ple, w_ple_gate,
           ln_final):
    depth = p.shape[0]
    assert depth == 1, "this implementation covers the single-layer configuration"
    return _layer(x, p[0], positions, ln_ffn1[0], w_ffn1_gate[0], w_ffn1_up[0], w_ffn1_down[0], ln_mix[0], w_in[0],
                  b_forget[0], w_merge[0], b_merge[0], w_ret_out[0], w_fox_out[0], w_out[0], ln_ffn2[0],
                  w_ffn2_gate[0], w_ffn2_up[0], w_ffn2_down[0], ln_ple[0], w_ple[0], w_ple_gate[0], ln_final)
```

```python
import functools

import numpy as np
import jax
import jax.numpy as jnp
from jax import lax
from jax.experimental import pallas as pl
from jax.experimental.pallas import tpu as pltpu

F32 = jnp.float32
BF16 = jnp.bfloat16

EPS = 1e-6
ROPE_BASE = 10000.0
RET_HEADS = 4
RET_HEAD_DIM = 128
RET_WIDTH = RET_HEADS * RET_HEAD_DIM
FOX_HEADS = 8
FOX_HEAD_DIM = 64
FOX_WIDTH = FOX_HEADS * FOX_HEAD_DIM
RET_CHUNK = 128

LANES = 128
FF_CHUNK = 256
NEG_BIG = -1e30

TOKEN_TILE = 512
RET_TILE = 1024
FOX_TILE = 256
VMEM_MB = {"ffn": 48, "mix_in": 48, "retention": 32, "fox": 40, "post": 58}
LOG2E = 1.4426950408889634
FOX_ROUND = 8
FOX_UNROLL = 3
FOX_LOOKAHEAD = 2
FOX_ACC_ROWS = FOX_HEAD_DIM + 16

_NT = (((1,), (1,)), ((), ()))
_TN = (((0,), (0,)), ((), ()))


def _const_spec(shape):
    zeros = (0,) * len(shape)
    return pl.BlockSpec(shape, lambda *_: zeros, pipeline_mode=pl.Buffered(1))


def _params(n_axes, vmem_mb):
    return pltpu.CompilerParams(
        dimension_semantics=("arbitrary",) * n_axes,
        vmem_limit_bytes=vmem_mb * 1024 * 1024,
    )


def _rms_norm(x, g):
    return x * lax.rsqrt(jnp.mean(x * x, axis=-1, keepdims=True) + EPS) * g


def _swiglu_tile(xn, wg_ref, wu_ref, wd_ref):
    acc = None
    for c in range(wg_ref.shape[1] // FF_CHUNK):
        cols = slice(c * FF_CHUNK, (c + 1) * FF_CHUNK)
        g = jnp.dot(xn, wg_ref[:, cols], preferred_element_type=F32)
        u = jnp.dot(xn, wu_ref[:, cols], preferred_element_type=F32)
        a = (g * jax.nn.sigmoid(g) * u).astype(BF16)
        d = jnp.dot(a, wd_ref[cols, :], preferred_element_type=F32)
        acc = d if acc is None else acc + d
    return acc


def _ffn_kernel(x_ref, g_ref, wg_ref, wu_ref, wd_ref, o_ref):
    x = x_ref[...]
    xn = _rms_norm(x, g_ref[...]).astype(BF16)
    o_ref[...] = x + 0.5 * _swiglu_tile(xn, wg_ref, wu_ref, wd_ref)


def _ffn(x, g, wg, wu, wd, tm):
    t, d = x.shape
    tile = pl.BlockSpec((tm, d), lambda i: (i, 0))
    return pl.pallas_call(
        _ffn_kernel,
        grid=(t // tm,),
        in_specs=[tile, _const_spec(g.shape), _const_spec(wg.shape), _const_spec(wu.shape), _const_spec(wd.shape)],
        out_specs=tile,
        out_shape=jax.ShapeDtypeStruct((t, d), F32),
        compiler_params=_params(1, VMEM_MB["ffn"]),
        name="ffn",
    )(x, g, wg, wu, wd)


def _mix_in_kernel(h_ref, g_ref, pos_ref, invf_ref, sgn_ref, wr_ref, wfk_ref, wqvt_ref, bf_ref,
                   place_ref, wm_ref, bm_ref, r_ref, fk_ref, fqvt_ref, gate_ref, carry_ref, *,
                   ret_scale, fox_scale):
    tm = h_ref.shape[1]
    u = _rms_norm(h_ref[0], g_ref[...]).astype(BF16)

    gm = jnp.dot(u, wm_ref[...], preferred_element_type=F32) + bm_ref[...]
    gate_ref[0] = jax.nn.sigmoid(gm).astype(BF16)

    pr = jnp.dot(u, wr_ref[...], preferred_element_type=F32)
    qk0 = LANES
    ff = pr[:, :LANES] + bf_ref[...]
    logf = jnp.minimum(ff, 0.0) - jnp.log1p(jnp.exp(-jnp.abs(ff)))
    tok = lax.broadcasted_iota(jnp.int32, logf.shape, 0)
    cum = logf
    sh = 1
    while sh < tm:
        cum = cum + jnp.where(tok >= sh, pltpu.roll(cum, sh, axis=0), 0.0)
        sh *= 2

    @pl.when(pl.program_id(1) == 0)
    def _():
        carry_ref[...] = jnp.zeros_like(carry_ref)

    cum = cum + carry_ref[0:1, :]
    carry_ref[...] = jnp.broadcast_to(cum[tm - 1:tm, :], carry_ref.shape)

    c2 = cum * LOG2E
    c_hi = c2.astype(BF16).astype(F32)
    rest = c2 - c_hi
    c_mid = rest.astype(BF16).astype(F32)
    c_lo = rest - c_mid
    lane = lax.broadcasted_iota(jnp.int32, c2.shape, 1)
    pieces = jnp.where(lane < FOX_HEADS, c_hi,
                       jnp.where(lane < 2 * FOX_HEADS, pltpu.roll(c_mid, FOX_HEADS, axis=1),
                                 pltpu.roll(c_lo, 2 * FOX_HEADS, axis=1))).astype(BF16)

    first = lax.broadcasted_iota(jnp.int32, (tm // 2, LANES), 1) < RET_HEAD_DIM // 2
    pos = pos_ref[0]
    ang = jnp.where(first, pos[:tm // 2], pos[tm // 2:]).astype(F32) * invf_ref[...]
    cos_h, sin_h = jnp.cos(ang), jnp.sin(ang)
    cos_r, sin_r = pltpu.roll(cos_h, RET_HEAD_DIM // 2, axis=1), pltpu.roll(sin_h, RET_HEAD_DIM // 2, axis=1)
    cos = jnp.concatenate([jnp.where(first, cos_h, cos_r), jnp.where(first, cos_r, cos_h)], axis=0)
    sin = jnp.concatenate([jnp.where(first, sin_h, sin_r), jnp.where(first, sin_r, sin_h)], axis=0)
    sin = sin * sgn_ref[...]
    for part, scale in ((0, ret_scale), (1, 1.0)):
        for hd in range(RET_HEADS):
            lo = part * RET_WIDTH + hd * RET_HEAD_DIM
            xh = pr[:, qk0 + lo:qk0 + lo + RET_HEAD_DIM]
            rot = xh * cos + pltpu.roll(xh, RET_HEAD_DIM // 2, axis=1) * sin
            if scale != 1.0:
                rot = rot * scale
            r_ref[0, :, lo:lo + RET_HEAD_DIM] = rot.astype(BF16)
    r_ref[0, :, 2 * RET_WIDTH:3 * RET_WIDTH] = pr[:, qk0 + 2 * RET_WIDTH:qk0 + 3 * RET_WIDTH].astype(BF16)
    rg = pr[:, qk0 + 3 * RET_WIDTH:]
    r_ref[0, :, 3 * RET_WIDTH:] = (rg * jax.nn.sigmoid(rg)).astype(BF16)

    fk = jnp.dot(u, wfk_ref[...], preferred_element_type=F32).astype(BF16)
    qvt = lax.dot_general(wqvt_ref[...], u, _NT, preferred_element_type=F32)
    fqvt_ref[0, :FOX_WIDTH, :] = (qvt[:FOX_WIDTH] * fox_scale).astype(BF16)
    fqvt_ref[0, FOX_WIDTH:, :] = qvt[FOX_WIDTH:].astype(BF16)

    placed = jnp.dot(pieces, place_ref[...], preferred_element_type=F32).astype(BF16)
    for pr_i in range(FOX_WIDTH // LANES):
        fk_ref[0, :, 2 * pr_i * LANES:(2 * pr_i + 1) * LANES] = fk[:, pr_i * LANES:(pr_i + 1) * LANES]
        fk_ref[0, :, (2 * pr_i + 1) * LANES:(2 * pr_i + 2) * LANES] = placed[:, pr_i * LANES:(pr_i + 1) * LANES]


def _forget_placement():
    place = np.zeros((LANES, FOX_WIDTH), np.float32)
    for head in range(FOX_HEADS):
        for piece in range(3):
            place[piece * FOX_HEADS + head, (head // 2) * LANES + 3 * (head % 2) + piece] = 1.0
    return jnp.asarray(place, dtype=BF16)


def _mix_in(h, g, pos, invf, sgn, wr, wfk, wqvt, bfg, wm, bm, tm):
    b, s, d = h.shape
    kern = functools.partial(_mix_in_kernel, ret_scale=RET_HEAD_DIM ** -0.5,
                             fox_scale=FOX_HEAD_DIM ** -0.5 * LOG2E)
    place = _forget_placement()
    tile = lambda w: pl.BlockSpec((1, tm, w), lambda i, j: (i, j, 0))
    return pl.pallas_call(
        kern,
        grid=(b, s // tm),
        in_specs=[tile(d), _const_spec(g.shape), tile(1), _const_spec(invf.shape), _const_spec(sgn.shape),
                  _const_spec(wr.shape),
                  _const_spec(wfk.shape), _const_spec(wqvt.shape), _const_spec(bfg.shape),
                  _const_spec(place.shape), _const_spec(wm.shape), _const_spec(bm.shape)],
        out_specs=[tile(4 * RET_WIDTH), tile(2 * FOX_WIDTH),
                   pl.BlockSpec((1, 2 * FOX_WIDTH, tm), lambda i, j: (i, 0, j)), tile(2 * d)],
        out_shape=[jax.ShapeDtypeStruct((b, s, 4 * RET_WIDTH), BF16),
                   jax.ShapeDtypeStruct((b, s, 2 * FOX_WIDTH), BF16),
                   jax.ShapeDtypeStruct((b, 2 * FOX_WIDTH, s), BF16),
                   jax.ShapeDtypeStruct((b, s, 2 * d), BF16)],
        scratch_shapes=[pltpu.VMEM((8, LANES), F32)],
        compiler_params=_params(2, VMEM_MB["mix_in"]),
        name="mix_in",
    )(h, g, pos, invf, sgn, wr, wfk, wqvt, bfg, place, wm, bm)


def _retention_kernel(q_ref, k_ref, v_ref, g_ref, dmask_ref, qdec_ref, kdec_ref, o_ref, state_ref, *, chunk_decay):
    tr = q_ref.shape[1]

    @pl.when(pl.program_id(1) == 0)
    def _():
        state_ref[...] = jnp.zeros_like(state_ref)

    def tile(ref, c, hd):
        return ref[0, c * RET_CHUNK:(c + 1) * RET_CHUNK, hd * RET_HEAD_DIM:(hd + 1) * RET_HEAD_DIM]

    blocks = [(c, hd) for c in range(tr // RET_CHUNK) for hd in range(RET_HEADS)]
    qk, kv = {}, {}
    for c, hd in blocks:
        k = tile(k_ref, c, hd)
        qk[c, hd] = lax.dot_general(tile(q_ref, c, hd), k, _NT, preferred_element_type=F32)
        k_dec = (k.astype(F32) * kdec_ref[hd]).astype(BF16)
        kv[c, hd] = lax.dot_general(k_dec, tile(v_ref, c, hd), _TN, preferred_element_type=F32)
    states = [state_ref[hd] for hd in range(RET_HEADS)]
    for c, hd in blocks:
        q = tile(q_ref, c, hd)
        scores = (qk[c, hd] * dmask_ref[hd]).astype(BF16)
        q_dec = (q.astype(F32) * qdec_ref[hd]).astype(BF16)
        y = jnp.dot(jnp.concatenate([scores, q_dec], axis=1),
                    jnp.concatenate([tile(v_ref, c, hd), states[hd].astype(BF16)], axis=0),
                    preferred_element_type=F32)
        states[hd] = chunk_decay[hd] * states[hd] + kv[c, hd]
        mu = jnp.mean(y, axis=-1, keepdims=True)
        yc = y - mu
        var = jnp.mean(yc * yc, axis=-1, keepdims=True)
        yn = yc * lax.rsqrt(var + EPS)
        rows = slice(c * RET_CHUNK, (c + 1) * RET_CHUNK)
        cols = slice(hd * RET_HEAD_DIM, (hd + 1) * RET_HEAD_DIM)
        o_ref[0, rows, cols] = (yn * g_ref[0, rows, cols].astype(F32)).astype(BF16)
    for hd in range(RET_HEADS):
        state_ref[hd] = states[hd]


def _retention(r, dmask, qdec, kdec, chunk_decay, tr):
    b, s, _ = r.shape
    kern = functools.partial(_retention_kernel, chunk_decay=chunk_decay)
    part = lambda p: pl.BlockSpec((1, tr, RET_WIDTH), lambda i, j: (i, j, p))
    return pl.pallas_call(
        kern,
        grid=(b, s // tr),
        in_specs=[part(0), part(1), part(2), part(3),
                  _const_spec(dmask.shape), _const_spec(qdec.shape), _const_spec(kdec.shape)],
        out_specs=pl.BlockSpec((1, tr, RET_WIDTH), lambda i, j: (i, j, 0)),
        out_shape=jax.ShapeDtypeStruct((b, s, RET_WIDTH), BF16),
        scratch_shapes=[pltpu.VMEM((RET_HEADS, RET_HEAD_DIM, RET_HEAD_DIM), F32)],
        compiler_params=_params(2, VMEM_MB["retention"]),
        name="retention",
    )(r, r, r, r, dmask, qdec, kdec)


def _fox_schedule(nq):
    remaining = {qi: list(range(qi)) for qi in range(1, nq)}
    rounds = []
    while any(remaining.values()):
        pick = sorted((q for q in remaining if remaining[q]), key=lambda q: (-len(remaining[q]), q))[:FOX_ROUND]
        rounds.append([(q, remaining[q].pop(), q) for q in pick])
    n_plain = len(rounds)
    diag = [(q, q, q) for q in range(nq)]
    rounds += [diag[i:i + FOX_ROUND] for i in range(0, nq, FOX_ROUND)]
    flat = [t for rnd in rounds for t in rnd + [(0, 0, nq)] * (FOX_ROUND - len(rnd))]
    tables = tuple(jnp.asarray([t[i] for t in flat], dtype=jnp.int32) for i in range(3))
    return tables, n_plain, len(rounds)


def _fox_kernel(qsrc_ref, ksrc_ref, slot_ref, qt_ref, k_ref, vt_ref, o_ref, m_ref, acc_ref, *,
                tq, n_plain, n_rounds):
    s_len = k_ref.shape[1]
    nq = s_len // tq

    m_ref[...] = jnp.full(m_ref.shape, NEG_BIG, F32)
    acc_ref[...] = jnp.zeros(acc_ref.shape, F32)

    key_idx = lax.broadcasted_iota(jnp.int32, (tq, tq), 0)
    qry_idx = lax.broadcasted_iota(jnp.int32, (tq, tq), 1)
    causal = key_idx <= qry_idx
    half_zero = jnp.zeros((FOX_HEAD_DIM, tq), BF16)
    ones_rows = jnp.ones((FOX_ACC_ROWS - FOX_HEAD_DIM, tq), BF16)
    piece_row = lax.broadcasted_iota(jnp.int32, (LANES, tq), 0)
    pick_c = [jnp.where((piece_row >= 3 * hd) & (piece_row < 3 * hd + 3), -1.0, 0.0).astype(BF16) for hd in range(2)]
    chains = [(t, hd) for t in range(FOX_ROUND) for hd in range(2)]

    def run_round(rnd, masked):
        slots = [slot_ref[rnd * FOX_ROUND + t] for t in range(FOX_ROUND)]
        k0s = [pl.multiple_of(ksrc_ref[rnd * FOX_ROUND + t] * tq, tq) for t in range(FOX_ROUND)]
        qk = {}

        def scores(t):
            q0 = pl.multiple_of(qsrc_ref[rnd * FOX_ROUND + t] * tq, tq)
            qt = qt_ref[0, :, pl.ds(q0, tq)]
            qts = (jnp.concatenate([qt[:FOX_HEAD_DIM], half_zero, pick_c[0]], axis=0),
                   jnp.concatenate([half_zero, qt[FOX_HEAD_DIM:], pick_c[1]], axis=0))
            k2 = k_ref[0, pl.ds(k0s[t], tq), :]
            for hd in range(2):
                qk[t, hd] = jnp.dot(k2, qts[hd], preferred_element_type=F32)

        for t in range(FOX_LOOKAHEAD):
            scores(t)
        for t, hd in chains:
            if hd == 0 and t + FOX_LOOKAHEAD < FOX_ROUND:
                scores(t + FOX_LOOKAHEAD)
            s = qk[t, hd]
            if masked:
                s = jnp.where(causal, s, NEG_BIG)
            m_old = m_ref[slots[t], hd]
            m_new = jnp.maximum(m_old, jnp.max(s, axis=0, keepdims=True))
            m_ref[slots[t], hd] = m_new
            p = jnp.exp2(s - m_new).astype(BF16)
            vt = vt_ref[0, hd * FOX_HEAD_DIM:(hd + 1) * FOX_HEAD_DIM, pl.ds(k0s[t], tq)]
            vt = jnp.concatenate([vt, ones_rows], axis=0)
            acc_ref[slots[t], hd] = (jnp.exp2(m_old - m_new) * acc_ref[slots[t], hd]
                                     + jnp.dot(vt, p, preferred_element_type=F32))
        return 0

    def sweep(lo, hi, masked):
        def rounds(i, _):
            for j in range(FOX_UNROLL):
                run_round(lo + FOX_UNROLL * i + j, masked)
            return 0
        lax.fori_loop(0, (hi - lo) // FOX_UNROLL, rounds, 0)
        for rnd in range(hi - (hi - lo) % FOX_UNROLL, hi):
            run_round(rnd, masked)

    sweep(0, n_plain, False)
    sweep(n_plain, n_rounds, True)

    def finish(qi, _):
        outs = []
        for hd in range(2):
            acc = acc_ref[qi, hd]
            outs.append(acc[:FOX_HEAD_DIM] / acc[FOX_HEAD_DIM:FOX_HEAD_DIM + 1])
        q0 = pl.multiple_of(qi * tq, tq)
        o_ref[0, :, pl.ds(q0, tq)] = jnp.concatenate(outs, axis=0).astype(BF16)
        return 0

    lax.fori_loop(0, nq, finish, 0)


def _fox(fk, fqvt, tq):
    b, s, _ = fk.shape
    nq = s // tq
    pairs = FOX_WIDTH // LANES
    tables, n_plain, n_rounds = _fox_schedule(nq)
    kern = functools.partial(_fox_kernel, tq=tq, n_plain=n_plain, n_rounds=n_rounds)
    grid_spec = pltpu.PrefetchScalarGridSpec(
        num_scalar_prefetch=len(tables),
        grid=(b, pairs),
        in_specs=[pl.BlockSpec((1, LANES, s), lambda i, j, *_: (i, j, 0)),
                  pl.BlockSpec((1, s, 2 * LANES), lambda i, j, *_: (i, 0, j)),
                  pl.BlockSpec((1, LANES, s), lambda i, j, *_: (i, pairs + j, 0))],
        out_specs=pl.BlockSpec((1, LANES, s), lambda i, j, *_: (i, j, 0)),
        scratch_shapes=[pltpu.VMEM((nq + 1, 2, 1, tq), F32),
                        pltpu.VMEM((nq + 1, 2, FOX_ACC_ROWS, tq), F32)],
    )
    return pl.pallas_call(
        kern,
        grid_spec=grid_spec,
        out_shape=jax.ShapeDtypeStruct((b, FOX_WIDTH, s), BF16),
        compiler_params=_params(2, VMEM_MB["fox"]),
        name="fox",
    )(*tables, fqvt, fk, fqvt)


def _post_kernel(h_ref, yr_ref, yf_ref, gate_ref, p_ref, wro_ref, wfo_ref, wo_ref, g2_ref, wg_ref, wu_ref, wd_ref,
                 gp_ref, wpg_ref, wp_ref, gf_ref, o_ref):
    d = h_ref.shape[1]
    z_a = jnp.dot(yr_ref[...], wro_ref[...], preferred_element_type=F32)
    z_b = lax.dot_general(yf_ref[0], wfo_ref[...], _TN, preferred_element_type=F32)
    mixed = gate_ref[:, :d].astype(F32) * z_a + gate_ref[:, d:].astype(F32) * z_b
    h = h_ref[...] + jnp.dot(mixed.astype(BF16), wo_ref[...], preferred_element_type=F32)
    emb = jnp.dot(p_ref[...].astype(BF16), wp_ref[...], preferred_element_type=F32)
    h = h + 0.5 * _swiglu_tile(_rms_norm(h, g2_ref[...]).astype(BF16), wg_ref, wu_ref, wd_ref)
    hn = _rms_norm(h, gp_ref[...]).astype(BF16)
    ple_gate = jax.nn.sigmoid(jnp.dot(hn, wpg_ref[...], preferred_element_type=F32))
    o_ref[...] = _rms_norm(h + ple_gate * emb, gf_ref[...])


def _post(h, yr, yf_t, gates, p, wro, wfo, wo, g2, wg, wu, wd, gp, wpg, wp, gf, tm):
    t, d = h.shape
    per_seq = yf_t.shape[2] // tm
    tile = lambda w: pl.BlockSpec((tm, w), lambda i: (i, 0))
    consts = [wro, wfo, wo, g2, wg, wu, wd, gp, wpg, wp, gf]
    return pl.pallas_call(
        _post_kernel,
        grid=(t // tm,),
        in_specs=[tile(d), tile(RET_WIDTH),
                  pl.BlockSpec((1, FOX_WIDTH, tm), lambda i: (i // per_seq, 0, i % per_seq)),
                  tile(2 * d), tile(p.shape[1])]
        + [_const_spec(c.shape) for c in consts],
        out_specs=tile(d),
        out_shape=jax.ShapeDtypeStruct((t, d), F32),
        compiler_params=_params(1, VMEM_MB["post"]),
        name="post",
    )(h, yr, yf_t, gates, p, *consts)


def _retention_tables():
    hd = np.arange(RET_HEADS, dtype=np.float64)
    log_gamma = np.log1p(-np.exp2(-5.0 - hd))
    idx = np.arange(RET_CHUNK, dtype=np.float64)
    diff = idx[:, None] - idx[None, :]
    dmask = np.where(diff >= 0, np.exp(log_gamma[:, None, None] * np.maximum(diff, 0.0)), 0.0)
    ones = np.ones((1, 1, RET_HEAD_DIM))
    qdec = np.exp(log_gamma[:, None] * (idx + 1.0))[:, :, None] * ones
    kdec = np.exp(log_gamma[:, None] * (RET_CHUNK - 1 - idx))[:, :, None] * ones
    chunk_decay = tuple(float(x) for x in np.exp(log_gamma * RET_CHUNK))
    as_f32 = lambda a: jnp.asarray(a, dtype=F32)
    return as_f32(dmask), as_f32(qdec), as_f32(kdec), chunk_decay


def _layer(h, p, positions, ln_ffn1, w_ffn1_gate, w_ffn1_up, w_ffn1_down, ln_mix, w_in, b_forget, w_merge,
           b_merge, w_ret_out, w_fox_out, w_out, ln_ffn2, w_ffn2_gate, w_ffn2_up, w_ffn2_down, ln_ple, w_ple,
           w_ple_gate, ln_final):
    b, s, d = h.shape
    t = b * s
    tm = min(TOKEN_TILE, s)
    row = lambda v: v.reshape(1, -1).astype(F32)

    inv_freq = 1.0 / (ROPE_BASE ** (jnp.arange(RET_HEAD_DIM // 2, dtype=F32) / (RET_HEAD_DIM // 2)))
    invf = jnp.concatenate([inv_freq, inv_freq]).reshape(1, RET_HEAD_DIM)
    sgn = jnp.concatenate([-jnp.ones((RET_HEAD_DIM // 2,), F32), jnp.ones((RET_HEAD_DIM // 2,), F32)]).reshape(1, -1)
    h1 = _ffn(h.reshape(t, d), row(ln_ffn1), w_ffn1_gate.astype(BF16), w_ffn1_up.astype(BF16),
              w_ffn1_down.astype(BF16), tm)

    even_odd = lambda w: w.reshape(d, RET_HEADS, RET_HEAD_DIM // 2, 2).transpose(0, 1, 3, 2).reshape(d, RET_WIDTH)
    w_rq = even_odd(w_in[:, :RET_WIDTH].astype(BF16))
    w_rk = even_odd(w_in[:, RET_WIDTH:2 * RET_WIDTH].astype(BF16))
    fox_lo = 4 * RET_WIDTH
    fox_hi = fox_lo + 3 * FOX_WIDTH
    w_ff = jnp.pad(w_in[:, fox_hi:], ((0, 0), (0, LANES - FOX_HEADS)))
    wr = jnp.concatenate([w_ff, w_rq, w_rk, w_in[:, 2 * RET_WIDTH:fox_lo]], axis=1).astype(BF16)
    w_fq, w_fk, w_fv = (w_in[:, fox_lo + i * FOX_WIDTH:fox_lo + (i + 1) * FOX_WIDTH] for i in range(3))
    wqvt = jnp.concatenate([w_fq, w_fv], axis=1).T.astype(BF16)
    bfg = jnp.pad(b_forget.astype(F32), (0, LANES - FOX_HEADS)).reshape(1, LANES)
    r, fk, fqvt, gates = _mix_in(h1.reshape(b, s, d), row(ln_mix), positions.reshape(b, s, 1), invf, sgn, wr,
                                 w_fk.astype(BF16), wqvt, bfg, w_merge.astype(BF16), row(b_merge), tm)

    dmask, qdec, kdec, chunk_decay = _retention_tables()
    y_ret = _retention(r, dmask, qdec, kdec, chunk_decay, min(RET_TILE, s))
    y_fox = _fox(fk, fqvt, min(FOX_TILE, s))

    assert ln_final is not None, "multi-layer stacks need a post kernel without the final norm"
    out = _post(h1, y_ret.reshape(t, RET_WIDTH), y_fox, gates.reshape(t, 2 * d),
                p.reshape(t, -1), w_ret_out.astype(BF16), w_fox_out.astype(BF16), w_out.astype(BF16), row(ln_ffn2),
                w_ffn2_gate.astype(BF16), w_ffn2_up.astype(BF16), w_ffn2_down.astype(BF16), row(ln_ple),
                w_ple_gate.astype(BF16), w_ple.astype(BF16), row(ln_final), tm)
    return out.reshape(b, s, d)


def kernel(x, p, positions, ln_ffn1, w_ffn1_gate, w_ffn1_up, w_ffn1_down, ln_mix, w_in, b_forget, w_merge, b_merge,
           w_ret_out, w_fox_out, w_out, ln_ffn2, w_ffn2_gate, w_ffn2_up, w_ffn2_down, ln_ple, w_ple, w_ple_gate,
           ln_final):
    depth = p.shape[0]
    assert depth == 1, "this implementation covers the single-layer configuration"
    return _layer(x, p[0], positions, ln_ffn1[0], w_ffn1_gate[0], w_ffn1_up[0], w_ffn1_down[0], ln_mix[0], w_in[0],
                  b_forget[0], w_merge[0], b_merge[0], w_ret_out[0], w_fox_out[0], w_out[0], ln_ffn2[0],
                  w_ffn2_gate[0], w_ffn2_up[0], w_ffn2_down[0], ln_ple[0], w_ple[0], w_ple_gate[0], ln_final)
```

```python
import functools

import numpy as np
import jax
import jax.numpy as jnp
from jax import lax
from jax.experimental import pallas as pl
from jax.experimental.pallas import tpu as pltpu

F32 = jnp.float32
BF16 = jnp.bfloat16

EPS = 1e-6
ROPE_BASE = 10000.0
RET_HEADS = 4
RET_HEAD_DIM = 128
RET_WIDTH = RET_HEADS * RET_HEAD_DIM
FOX_HEADS = 8
FOX_HEAD_DIM = 64
FOX_WIDTH = FOX_HEADS * FOX_HEAD_DIM
RET_CHUNK = 128

LANES = 128
FF_CHUNK = 256
NEG_BIG = -1e30

TOKEN_TILE = 512
RET_TILE = 1024
FOX_TILE = 256
VMEM_MB = {"ffn": 48, "mix_in": 48, "retention": 32, "fox": 40, "post": 58}
LOG2E = 1.4426950408889634
FOX_ROUND = 8
FOX_UNROLL = 3
FOX_LOOKAHEAD = 2
FOX_ACC_ROWS = FOX_HEAD_DIM + 16

_NT = (((1,), (1,)), ((), ()))
_TN = (((0,), (0,)), ((), ()))


def _const_spec(shape, index=None):
    index = (0,) * len(shape) if index is None else index
    return pl.BlockSpec(shape, lambda *_: index, pipeline_mode=pl.Buffered(1))


def _params(n_axes, vmem_mb):
    return pltpu.CompilerParams(
        dimension_semantics=("arbitrary",) * n_axes,
        vmem_limit_bytes=vmem_mb * 1024 * 1024,
    )


def _rms_norm(x, g):
    return x * lax.rsqrt(jnp.mean(x * x, axis=-1, keepdims=True) + EPS) * g


def _swiglu_tile(xn, wgu_ref, wd_ref):
    acc = None
    for c in range(wgu_ref.shape[2] // FF_CHUNK):
        cols = slice(c * FF_CHUNK, (c + 1) * FF_CHUNK)
        g = jnp.dot(xn, wgu_ref[0, :, cols], preferred_element_type=F32)
        u = jnp.dot(xn, wgu_ref[1, :, cols], preferred_element_type=F32)
        a = (g * jax.nn.sigmoid(g) * u).astype(BF16)
        d = jnp.dot(a, wd_ref[0, cols, :], preferred_element_type=F32)
        acc = d if acc is None else acc + d
    return acc


def _ffn_kernel(x_ref, g_ref, wgu_ref, wd_ref, o_ref):
    x = x_ref[...]
    xn = _rms_norm(x, g_ref[...]).astype(BF16)
    o_ref[...] = x + 0.5 * _swiglu_tile(xn, wgu_ref, wd_ref)


def _ffn_weight_specs(wgu, wd, which):
    return [_const_spec((2,) + wgu.shape[1:], (which, 0, 0)), _const_spec((1,) + wd.shape[1:], (which, 0, 0))]


def _ffn(x, g, wgu, wd, which, tm):
    t, d = x.shape
    tile = pl.BlockSpec((tm, d), lambda i: (i, 0))
    return pl.pallas_call(
        _ffn_kernel,
        grid=(t // tm,),
        in_specs=[tile, _const_spec(g.shape)] + _ffn_weight_specs(wgu, wd, which),
        out_specs=tile,
        out_shape=jax.ShapeDtypeStruct((t, d), F32),
        compiler_params=_params(1, VMEM_MB["ffn"]),
        name="ffn",
    )(x, g, wgu, wd)


def _mix_in_kernel(h_ref, g_ref, pos_ref, invf_ref, sgn_ref, wr_ref, wfk_ref, wqvt_ref, bf_ref,
                   place_ref, wm_ref, bm_ref, r_ref, fk_ref, fqvt_ref, gate_ref, carry_ref, *,
                   ret_scale, fox_scale):
    tm = h_ref.shape[1]
    u = _rms_norm(h_ref[0], g_ref[...]).astype(BF16)

    gm = jnp.dot(u, wm_ref[...], preferred_element_type=F32) + bm_ref[...]
    gate_ref[0] = jax.nn.sigmoid(gm).astype(BF16)

    pr = jnp.dot(u, wr_ref[...], preferred_element_type=F32)
    qk0 = LANES
    ff = pr[:, :LANES] + bf_ref[...]
    logf = jnp.minimum(ff, 0.0) - jnp.log1p(jnp.exp(-jnp.abs(ff)))
    tok = lax.broadcasted_iota(jnp.int32, logf.shape, 0)
    cum = logf
    sh = 1
    while sh < tm:
        cum = cum + jnp.where(tok >= sh, pltpu.roll(cum, sh, axis=0), 0.0)
        sh *= 2

    @pl.when(pl.program_id(1) == 0)
    def _():
        carry_ref[...] = jnp.zeros_like(carry_ref)

    cum = cum + carry_ref[0:1, :]
    carry_ref[...] = jnp.broadcast_to(cum[tm - 1:tm, :], carry_ref.shape)

    c2 = cum * LOG2E
    c_hi = c2.astype(BF16).astype(F32)
    rest = c2 - c_hi
    c_mid = rest.astype(BF16).astype(F32)
    c_lo = rest - c_mid
    lane = lax.broadcasted_iota(jnp.int32, c2.shape, 1)
    pieces = jnp.where(lane < FOX_HEADS, c_hi,
                       jnp.where(lane < 2 * FOX_HEADS, pltpu.roll(c_mid, FOX_HEADS, axis=1),
                                 pltpu.roll(c_lo, 2 * FOX_HEADS, axis=1))).astype(BF16)

    first = lax.broadcasted_iota(jnp.int32, (tm // 2, LANES), 1) < RET_HEAD_DIM // 2
    pos = pos_ref[0]
    ang = jnp.where(first, pos[:tm // 2], pos[tm // 2:]).astype(F32) * invf_ref[...]
    cos_h, sin_h = jnp.cos(ang), jnp.sin(ang)
    cos_r, sin_r = pltpu.roll(cos_h, RET_HEAD_DIM // 2, axis=1), pltpu.roll(sin_h, RET_HEAD_DIM // 2, axis=1)
    cos = jnp.concatenate([jnp.where(first, cos_h, cos_r), jnp.where(first, cos_r, cos_h)], axis=0)
    sin = jnp.concatenate([jnp.where(first, sin_h, sin_r), jnp.where(first, sin_r, sin_h)], axis=0)
    sin = sin * sgn_ref[...]
    for part, scale in ((0, ret_scale), (1, 1.0)):
        for hd in range(RET_HEADS):
            lo = part * RET_WIDTH + hd * RET_HEAD_DIM
            xh = pr[:, qk0 + lo:qk0 + lo + RET_HEAD_DIM]
            rot = xh * cos + pltpu.roll(xh, RET_HEAD_DIM // 2, axis=1) * sin
            if scale != 1.0:
                rot = rot * scale
            r_ref[0, :, lo:lo + RET_HEAD_DIM] = rot.astype(BF16)
    r_ref[0, :, 2 * RET_WIDTH:3 * RET_WIDTH] = pr[:, qk0 + 2 * RET_WIDTH:qk0 + 3 * RET_WIDTH].astype(BF16)
    rg = pr[:, qk0 + 3 * RET_WIDTH:]
    r_ref[0, :, 3 * RET_WIDTH:] = (rg * jax.nn.sigmoid(rg)).astype(BF16)

    fk = jnp.dot(u, wfk_ref[...], preferred_element_type=F32).astype(BF16)
    qvt = lax.dot_general(wqvt_ref[...], u, _NT, preferred_element_type=F32)
    fqvt_ref[0, :FOX_WIDTH, :] = (qvt[:FOX_WIDTH] * fox_scale).astype(BF16)
    fqvt_ref[0, FOX_WIDTH:, :] = qvt[FOX_WIDTH:].astype(BF16)

    placed = jnp.dot(pieces, place_ref[...], preferred_element_type=F32).astype(BF16)
    for pr_i in range(FOX_WIDTH // LANES):
        fk_ref[0, :, 2 * pr_i * LANES:(2 * pr_i + 1) * LANES] = fk[:, pr_i * LANES:(pr_i + 1) * LANES]
        fk_ref[0, :, (2 * pr_i + 1) * LANES:(2 * pr_i + 2) * LANES] = placed[:, pr_i * LANES:(pr_i + 1) * LANES]


def _forget_placement():
    place = np.zeros((LANES, FOX_WIDTH), np.float32)
    for head in range(FOX_HEADS):
        for piece in range(3):
            place[piece * FOX_HEADS + head, (head // 2) * LANES + 3 * (head % 2) + piece] = 1.0
    return jnp.asarray(place, dtype=BF16)


def _mix_in(h, g, pos, invf, sgn, wr, wfk, wqvt, bfg, wm, bm, tm):
    b, s, d = h.shape
    kern = functools.partial(_mix_in_kernel, ret_scale=RET_HEAD_DIM ** -0.5,
                             fox_scale=FOX_HEAD_DIM ** -0.5 * LOG2E)
    place = _forget_placement()
    tile = lambda w: pl.BlockSpec((1, tm, w), lambda i, j: (i, j, 0))
    return pl.pallas_call(
        kern,
        grid=(b, s // tm),
        in_specs=[tile(d), _const_spec(g.shape), tile(1), _const_spec(invf.shape), _const_spec(sgn.shape),
                  _const_spec(wr.shape),
                  _const_spec(wfk.shape), _const_spec(wqvt.shape), _const_spec(bfg.shape),
                  _const_spec(place.shape), _const_spec(wm.shape), _const_spec(bm.shape)],
        out_specs=[tile(4 * RET_WIDTH), tile(2 * FOX_WIDTH),
                   pl.BlockSpec((1, 2 * FOX_WIDTH, tm), lambda i, j: (i, 0, j)), tile(2 * d)],
        out_shape=[jax.ShapeDtypeStruct((b, s, 4 * RET_WIDTH), BF16),
                   jax.ShapeDtypeStruct((b, s, 2 * FOX_WIDTH), BF16),
                   jax.ShapeDtypeStruct((b, 2 * FOX_WIDTH, s), BF16),
                   jax.ShapeDtypeStruct((b, s, 2 * d), BF16)],
        scratch_shapes=[pltpu.VMEM((8, LANES), F32)],
        compiler_params=_params(2, VMEM_MB["mix_in"]),
        name="mix_in",
    )(h, g, pos, invf, sgn, wr, wfk, wqvt, bfg, place, wm, bm)


def _retention_kernel(q_ref, k_ref, v_ref, g_ref, dmask_ref, qdec_ref, kdec_ref, o_ref, state_ref, *, chunk_decay):
    tr = q_ref.shape[1]

    @pl.when(pl.program_id(1) == 0)
    def _():
        state_ref[...] = jnp.zeros_like(state_ref)

    def tile(ref, c, hd):
        return ref[0, c * RET_CHUNK:(c + 1) * RET_CHUNK, hd * RET_HEAD_DIM:(hd + 1) * RET_HEAD_DIM]

    blocks = [(c, hd) for c in range(tr // RET_CHUNK) for hd in range(RET_HEADS)]
    qk, kv = {}, {}
    for c, hd in blocks:
        k = tile(k_ref, c, hd)
        qk[c, hd] = lax.dot_general(tile(q_ref, c, hd), k, _NT, preferred_element_type=F32)
        k_dec = (k.astype(F32) * kdec_ref[hd]).astype(BF16)
        kv[c, hd] = lax.dot_general(k_dec, tile(v_ref, c, hd), _TN, preferred_element_type=F32)
    states = [state_ref[hd] for hd in range(RET_HEADS)]
    for c, hd in blocks:
        q = tile(q_ref, c, hd)
        scores = (qk[c, hd] * dmask_ref[hd]).astype(BF16)
        q_dec = (q.astype(F32) * qdec_ref[hd]).astype(BF16)
        y = jnp.dot(jnp.concatenate([scores, q_dec], axis=1),
                    jnp.concatenate([tile(v_ref, c, hd), states[hd].astype(BF16)], axis=0),
                    preferred_element_type=F32)
        states[hd] = chunk_decay[hd] * states[hd] + kv[c, hd]
        mu = jnp.mean(y, axis=-1, keepdims=True)
        yc = y - mu
        var = jnp.mean(yc * yc, axis=-1, keepdims=True)
        yn = yc * lax.rsqrt(var + EPS)
        rows = slice(c * RET_CHUNK, (c + 1) * RET_CHUNK)
        cols = slice(hd * RET_HEAD_DIM, (hd + 1) * RET_HEAD_DIM)
        o_ref[0, rows, cols] = (yn * g_ref[0, rows, cols].astype(F32)).astype(BF16)
    for hd in range(RET_HEADS):
        state_ref[hd] = states[hd]


def _retention(r, dmask, qdec, kdec, chunk_decay, tr):
    b, s, _ = r.shape
    kern = functools.partial(_retention_kernel, chunk_decay=chunk_decay)
    part = lambda p: pl.BlockSpec((1, tr, RET_WIDTH), lambda i, j: (i, j, p))
    return pl.pallas_call(
        kern,
        grid=(b, s // tr),
        in_specs=[part(0), part(1), part(2), part(3),
                  _const_spec(dmask.shape), _const_spec(qdec.shape), _const_spec(kdec.shape)],
        out_specs=pl.BlockSpec((1, tr, RET_WIDTH), lambda i, j: (i, j, 0)),
        out_shape=jax.ShapeDtypeStruct((b, s, RET_WIDTH), BF16),
        scratch_shapes=[pltpu.VMEM((RET_HEADS, RET_HEAD_DIM, RET_HEAD_DIM), F32)],
        compiler_params=_params(2, VMEM_MB["retention"]),
        name="retention",
    )(r, r, r, r, dmask, qdec, kdec)


def _fox_schedule(nq):
    remaining = {qi: list(range(qi)) for qi in range(1, nq)}
    rounds = []
    while any(remaining.values()):
        pick = sorted((q for q in remaining if remaining[q]), key=lambda q: (-len(remaining[q]), q))[:FOX_ROUND]
        rounds.append([(q, remaining[q].pop(), q) for q in pick])
    n_plain = len(rounds)
    diag = [(q, q, q) for q in range(nq)]
    rounds += [diag[i:i + FOX_ROUND] for i in range(0, nq, FOX_ROUND)]
    flat = [t for rnd in rounds for t in rnd + [(0, 0, nq)] * (FOX_ROUND - len(rnd))]
    tables = tuple(jnp.asarray([t[i] for t in flat], dtype=jnp.int32) for i in range(3))
    return tables, n_plain, len(rounds)


def _fox_kernel(qsrc_ref, ksrc_ref, slot_ref, qt_ref, k_ref, vt_ref, o_ref, m_ref, acc_ref, *,
                tq, n_plain, n_rounds):
    s_len = k_ref.shape[1]
    nq = s_len // tq

    m_ref[...] = jnp.full(m_ref.shape, NEG_BIG, F32)
    acc_ref[...] = jnp.zeros(acc_ref.shape, F32)

    key_idx = lax.broadcasted_iota(jnp.int32, (tq, tq), 0)
    qry_idx = lax.broadcasted_iota(jnp.int32, (tq, tq), 1)
    causal = key_idx <= qry_idx
    half_zero = jnp.zeros((FOX_HEAD_DIM, tq), BF16)
    ones_rows = jnp.ones((FOX_ACC_ROWS - FOX_HEAD_DIM, tq), BF16)
    piece_row = lax.broadcasted_iota(jnp.int32, (LANES, tq), 0)
    pick_c = [jnp.where((piece_row >= 3 * hd) & (piece_row < 3 * hd + 3), -1.0, 0.0).astype(BF16) for hd in range(2)]
    chains = [(t, hd) for t in range(FOX_ROUND) for hd in range(2)]

    def run_round(rnd, masked):
        slots = [slot_ref[rnd * FOX_ROUND + t] for t in range(FOX_ROUND)]
        k0s = [pl.multiple_of(ksrc_ref[rnd * FOX_ROUND + t] * tq, tq) for t in range(FOX_ROUND)]
        qk = {}

        def scores(t):
            q0 = pl.multiple_of(qsrc_ref[rnd * FOX_ROUND + t] * tq, tq)
            qt = qt_ref[0, :, pl.ds(q0, tq)]
            qts = (jnp.concatenate([qt[:FOX_HEAD_DIM], half_zero, pick_c[0]], axis=0),
                   jnp.concatenate([half_zero, qt[FOX_HEAD_DIM:], pick_c[1]], axis=0))
            k2 = k_ref[0, pl.ds(k0s[t], tq), :]
            for hd in range(2):
                qk[t, hd] = jnp.dot(k2, qts[hd], preferred_element_type=F32)

        for t in range(FOX_LOOKAHEAD):
            scores(t)
        for t, hd in chains:
            if hd == 0 and t + FOX_LOOKAHEAD < FOX_ROUND:
                scores(t + FOX_LOOKAHEAD)
            s = qk[t, hd]
            if masked:
                s = jnp.where(causal, s, NEG_BIG)
            m_old = m_ref[slots[t], hd]
            m_new = jnp.maximum(m_old, jnp.max(s, axis=0, keepdims=True))
            m_ref[slots[t], hd] = m_new
            p = jnp.exp2(s - m_new).astype(BF16)
            vt = vt_ref[0, hd * FOX_HEAD_DIM:(hd + 1) * FOX_HEAD_DIM, pl.ds(k0s[t], tq)]
            vt = jnp.concatenate([vt, ones_rows], axis=0)
            acc_ref[slots[t], hd] = (jnp.exp2(m_old - m_new) * acc_ref[slots[t], hd]
                                     + jnp.dot(vt, p, preferred_element_type=F32))
        return 0

    def sweep(lo, hi, masked):
        def rounds(i, _):
            for j in range(FOX_UNROLL):
                run_round(lo + FOX_UNROLL * i + j, masked)
            return 0
        lax.fori_loop(0, (hi - lo) // FOX_UNROLL, rounds, 0)
        for rnd in range(hi - (hi - lo) % FOX_UNROLL, hi):
            run_round(rnd, masked)

    sweep(0, n_plain, False)
    sweep(n_plain, n_rounds, True)

    def finish(qi, _):
        outs = []
        for hd in range(2):
            acc = acc_ref[qi, hd]
            outs.append(acc[:FOX_HEAD_DIM] / acc[FOX_HEAD_DIM:FOX_HEAD_DIM + 1])
        q0 = pl.multiple_of(qi * tq, tq)
        o_ref[0, :, pl.ds(q0, tq)] = jnp.concatenate(outs, axis=0).astype(BF16)
        return 0

    lax.fori_loop(0, nq, finish, 0)


def _fox(fk, fqvt, tq):
    b, s, _ = fk.shape
    nq = s // tq
    pairs = FOX_WIDTH // LANES
    tables, n_plain, n_rounds = _fox_schedule(nq)
    kern = functools.partial(_fox_kernel, tq=tq, n_plain=n_plain, n_rounds=n_rounds)
    grid_spec = pltpu.PrefetchScalarGridSpec(
        num_scalar_prefetch=len(tables),
        grid=(b, pairs),
        in_specs=[pl.BlockSpec((1, LANES, s), lambda i, j, *_: (i, j, 0)),
                  pl.BlockSpec((1, s, 2 * LANES), lambda i, j, *_: (i, 0, j)),
                  pl.BlockSpec((1, LANES, s), lambda i, j, *_: (i, pairs + j, 0))],
        out_specs=pl.BlockSpec((1, LANES, s), lambda i, j, *_: (i, j, 0)),
        scratch_shapes=[pltpu.VMEM((nq + 1, 2, 1, tq), F32),
                        pltpu.VMEM((nq + 1, 2, FOX_ACC_ROWS, tq), F32)],
    )
    return pl.pallas_call(
        kern,
        grid_spec=grid_spec,
        out_shape=jax.ShapeDtypeStruct((b, FOX_WIDTH, s), BF16),
        compiler_params=_params(2, VMEM_MB["fox"]),
        name="fox",
    )(*tables, fqvt, fk, fqvt)


def _post_kernel(h_ref, yr_ref, yf_ref, gate_ref, p_ref, wbr_ref, wsq_ref, g2_ref, wgu_ref, wd_ref,
                 gp_ref, wp_ref, gf_ref, o_ref):
    d = h_ref.shape[1]
    z_a = jnp.dot(yr_ref[...], wbr_ref[0], preferred_element_type=F32)
    z_b = lax.dot_general(yf_ref[0], wbr_ref[1], _TN, preferred_element_type=F32)
    mixed = gate_ref[:, :d].astype(F32) * z_a + gate_ref[:, d:].astype(F32) * z_b
    h = h_ref[...] + jnp.dot(mixed.astype(BF16), wsq_ref[0], preferred_element_type=F32)
    emb = jnp.dot(p_ref[...].astype(BF16), wp_ref[...], preferred_element_type=F32)
    h = h + 0.5 * _swiglu_tile(_rms_norm(h, g2_ref[...]).astype(BF16), wgu_ref, wd_ref)
    hn = _rms_norm(h, gp_ref[...]).astype(BF16)
    ple_gate = jax.nn.sigmoid(jnp.dot(hn, wsq_ref[1], preferred_element_type=F32))
    o_ref[...] = _rms_norm(h + ple_gate * emb, gf_ref[...])


def _post(h, yr, yf_t, gates, p, wbr, wsq, g2, wgu, wd, which, gp, wp, gf, tm):
    t, d = h.shape
    per_seq = yf_t.shape[2] // tm
    tile = lambda w: pl.BlockSpec((tm, w), lambda i: (i, 0))
    whole = lambda a: _const_spec(a.shape)
    return pl.pallas_call(
        _post_kernel,
        grid=(t // tm,),
        in_specs=[tile(d), tile(RET_WIDTH),
                  pl.BlockSpec((1, FOX_WIDTH, tm), lambda i: (i // per_seq, 0, i % per_seq)),
                  tile(2 * d), tile(p.shape[1]), whole(wbr), whole(wsq), whole(g2)]
        + _ffn_weight_specs(wgu, wd, which) + [whole(gp), whole(wp), whole(gf)],
        out_specs=tile(d),
        out_shape=jax.ShapeDtypeStruct((t, d), F32),
        compiler_params=_params(1, VMEM_MB["post"]),
        name="post",
    )(h, yr, yf_t, gates, p, wbr, wsq, g2, wgu, wd, gp, wp, gf)


def _retention_tables():
    hd = np.arange(RET_HEADS, dtype=np.float64)
    log_gamma = np.log1p(-np.exp2(-5.0 - hd))
    idx = np.arange(RET_CHUNK, dtype=np.float64)
    diff = idx[:, None] - idx[None, :]
    dmask = np.where(diff >= 0, np.exp(log_gamma[:, None, None] * np.maximum(diff, 0.0)), 0.0)
    ones = np.ones((1, 1, RET_HEAD_DIM))
    qdec = np.exp(log_gamma[:, None] * (idx + 1.0))[:, :, None] * ones
    kdec = np.exp(log_gamma[:, None] * (RET_CHUNK - 1 - idx))[:, :, None] * ones
    chunk_decay = tuple(float(x) for x in np.exp(log_gamma * RET_CHUNK))
    as_f32 = lambda a: jnp.asarray(a, dtype=F32)
    return as_f32(dmask), as_f32(qdec), as_f32(kdec), chunk_decay


def _layer(h, p, positions, ln_ffn1, w_ffn1_gate, w_ffn1_up, w_ffn1_down, ln_mix, w_in, b_forget, w_merge,
           b_merge, w_ret_out, w_fox_out, w_out, ln_ffn2, w_ffn2_gate, w_ffn2_up, w_ffn2_down, ln_ple, w_ple,
           w_ple_gate, ln_final):
    b, s, d = h.shape
    t = b * s
    tm = min(TOKEN_TILE, s)
    row = lambda v: v.reshape(1, -1).astype(F32)

    inv_freq = 1.0 / (ROPE_BASE ** (jnp.arange(RET_HEAD_DIM // 2, dtype=F32) / (RET_HEAD_DIM // 2)))
    invf = jnp.concatenate([inv_freq, inv_freq]).reshape(1, RET_HEAD_DIM)
    sgn = jnp.concatenate([-jnp.ones((RET_HEAD_DIM // 2,), F32), jnp.ones((RET_HEAD_DIM // 2,), F32)]).reshape(1, -1)
    wgu = jnp.stack([w_ffn1_gate, w_ffn1_up, w_ffn2_gate, w_ffn2_up]).astype(BF16)
    wd = jnp.stack([w_ffn1_down, w_ffn2_down]).astype(BF16)
    h1 = _ffn(h.reshape(t, d), row(ln_ffn1), wgu, wd, 0, tm)

    even_odd = lambda w: w.reshape(d, RET_HEADS, RET_HEAD_DIM // 2, 2).transpose(0, 1, 3, 2).reshape(d, RET_WIDTH)
    w_rq = even_odd(w_in[:, :RET_WIDTH].astype(BF16))
    w_rk = even_odd(w_in[:, RET_WIDTH:2 * RET_WIDTH].astype(BF16))
    fox_lo = 4 * RET_WIDTH
    fox_hi = fox_lo + 3 * FOX_WIDTH
    w_ff = jnp.pad(w_in[:, fox_hi:], ((0, 0), (0, LANES - FOX_HEADS)))
    wr = jnp.concatenate([w_ff, w_rq, w_rk, w_in[:, 2 * RET_WIDTH:fox_lo]], axis=1).astype(BF16)
    w_fq, w_fk, w_fv = (w_in[:, fox_lo + i * FOX_WIDTH:fox_lo + (i + 1) * FOX_WIDTH] for i in range(3))
    wqvt = jnp.concatenate([w_fq, w_fv], axis=1).T.astype(BF16)
    bfg = jnp.pad(b_forget.astype(F32), (0, LANES - FOX_HEADS)).reshape(1, LANES)
    r, fk, fqvt, gates = _mix_in(h1.reshape(b, s, d), row(ln_mix), positions.reshape(b, s, 1), invf, sgn, wr,
                                 w_fk.astype(BF16), wqvt, bfg, w_merge.astype(BF16), row(b_merge), tm)

    dmask, qdec, kdec, chunk_decay = _retention_tables()
    y_ret = _retention(r, dmask, qdec, kdec, chunk_decay, min(RET_TILE, s))
    y_fox = _fox(fk, fqvt, min(FOX_TILE, s))

    assert ln_final is not None, "multi-layer stacks need a post kernel without the final norm"
    out = _post(h1, y_ret.reshape(t, RET_WIDTH), y_fox, gates.reshape(t, 2 * d), p.reshape(t, -1),
                jnp.stack([w_ret_out, w_fox_out]).astype(BF16), jnp.stack([w_out, w_ple_gate]).astype(BF16),
                row(ln_ffn2), wgu, wd, 1, row(ln_ple), w_ple.astype(BF16), row(ln_final), tm)
    return out.reshape(b, s, d)


def kernel(x, p, positions, ln_ffn1, w_ffn1_gate, w_ffn1_up, w_ffn1_down, ln_mix, w_in, b_forget, w_merge, b_merge,
           w_ret_out, w_fox_out, w_out, ln_ffn2, w_ffn2_gate, w_ffn2_up, w_ffn2_down, ln_ple, w_ple, w_ple_gate,
           ln_final):
    depth = p.shape[0]
    assert depth == 1, "this implementation covers the single-layer configuration"
    return _layer(x, p[0], positions, ln_ffn1[0], w_ffn1_gate[0], w_ffn1_up[0], w_ffn1_down[0], ln_mix[0], w_in[0],
                  b_forget[0], w_merge[0], b_merge[0], w_ret_out[0], w_fox_out[0], w_out[0], ln_ffn2[0],
                  w_ffn2_gate[0], w_ffn2_up[0], w_ffn2_down[0], ln_ple[0], w_ple[0], w_ple_gate[0], ln_final)
```

```python
import functools

import numpy as np
import jax
import jax.numpy as jnp
from jax import lax
from jax.experimental import pallas as pl
from jax.experimental.pallas import tpu as pltpu

F32 = jnp.float32
BF16 = jnp.bfloat16

EPS = 1e-6
ROPE_BASE = 10000.0
RET_HEADS = 4
RET_HEAD_DIM = 128
RET_WIDTH = RET_HEADS * RET_HEAD_DIM
FOX_HEADS = 8
FOX_HEAD_DIM = 64
FOX_WIDTH = FOX_HEADS * FOX_HEAD_DIM
RET_CHUNK = 128

LANES = 128
FF_CHUNK = 256
NEG_BIG = -1e30

TOKEN_TILE = 512
RET_TILE = 2048
FOX_TILE = 256
VMEM_MB = {"ffn": 48, "mix_in": 48, "retention": 32, "fox": 40, "post": 58}
LOG2E = 1.4426950408889634
FOX_ROUND = 8
FOX_UNROLL = 5
FOX_LOOKAHEAD = 2
FOX_ACC_ROWS = FOX_HEAD_DIM + 16

_NT = (((1,), (1,)), ((), ()))
_TN = (((0,), (0,)), ((), ()))


def _const_spec(shape):
    zeros = (0,) * len(shape)
    return pl.BlockSpec(shape, lambda *_: zeros, pipeline_mode=pl.Buffered(1))


def _params(n_axes, vmem_mb):
    return pltpu.CompilerParams(
        dimension_semantics=("arbitrary",) * n_axes,
        vmem_limit_bytes=vmem_mb * 1024 * 1024,
    )


def _rms_norm(x, g):
    return x * lax.rsqrt(jnp.mean(x * x, axis=-1, keepdims=True) + EPS) * g


def _swiglu_tile(xn, wg_ref, wu_ref, wd_ref):
    acc = None
    for c in range(wg_ref.shape[1] // FF_CHUNK):
        cols = slice(c * FF_CHUNK, (c + 1) * FF_CHUNK)
        g = jnp.dot(xn, wg_ref[:, cols], preferred_element_type=F32)
        u = jnp.dot(xn, wu_ref[:, cols], preferred_element_type=F32)
        a = (g * jax.nn.sigmoid(g) * u).astype(BF16)
        d = jnp.dot(a, wd_ref[cols, :], preferred_element_type=F32)
        acc = d if acc is None else acc + d
    return acc


def _ffn_kernel(x_ref, g_ref, wg_ref, wu_ref, wd_ref, o_ref):
    x = x_ref[...]
    xn = _rms_norm(x, g_ref[...]).astype(BF16)
    o_ref[...] = x + 0.5 * _swiglu_tile(xn, wg_ref, wu_ref, wd_ref)


def _ffn(x, g, wg, wu, wd, tm):
    t, d = x.shape
    tile = pl.BlockSpec((tm, d), lambda i: (i, 0))
    return pl.pallas_call(
        _ffn_kernel,
        grid=(t // tm,),
        in_specs=[tile, _const_spec(g.shape), _const_spec(wg.shape), _const_spec(wu.shape), _const_spec(wd.shape)],
        out_specs=tile,
        out_shape=jax.ShapeDtypeStruct((t, d), F32),
        compiler_params=_params(1, VMEM_MB["ffn"]),
        name="ffn",
    )(x, g, wg, wu, wd)


def _mix_in_kernel(h_ref, g_ref, pos_ref, invf_ref, sgn_ref, wr_ref, wfk_ref, wqvt_ref, bf_ref,
                   place_ref, wm_ref, bm_ref, r_ref, fk_ref, fqvt_ref, gate_ref, carry_ref, *,
                   ret_scale, fox_scale):
    tm = h_ref.shape[1]
    u = _rms_norm(h_ref[0], g_ref[...]).astype(BF16)

    gm = jnp.dot(u, wm_ref[...], preferred_element_type=F32) + bm_ref[...]
    gate_ref[0] = jax.nn.sigmoid(gm).astype(BF16)

    pr = jnp.dot(u, wr_ref[...], preferred_element_type=F32)
    qk0 = LANES
    ff = pr[:, :LANES] + bf_ref[...]
    logf = jnp.minimum(ff, 0.0) - jnp.log1p(jnp.exp(-jnp.abs(ff)))
    tok = lax.broadcasted_iota(jnp.int32, logf.shape, 0)
    cum = logf
    sh = 1
    while sh < tm:
        cum = cum + jnp.where(tok >= sh, pltpu.roll(cum, sh, axis=0), 0.0)
        sh *= 2

    @pl.when(pl.program_id(1) == 0)
    def _():
        carry_ref[...] = jnp.zeros_like(carry_ref)

    cum = cum + carry_ref[0:1, :]
    carry_ref[...] = jnp.broadcast_to(cum[tm - 1:tm, :], carry_ref.shape)

    c2 = cum * LOG2E
    c_hi = c2.astype(BF16).astype(F32)
    rest = c2 - c_hi
    c_mid = rest.astype(BF16).astype(F32)
    c_lo = rest - c_mid
    lane = lax.broadcasted_iota(jnp.int32, c2.shape, 1)
    pieces = jnp.where(lane < FOX_HEADS, c_hi,
                       jnp.where(lane < 2 * FOX_HEADS, pltpu.roll(c_mid, FOX_HEADS, axis=1),
                                 pltpu.roll(c_lo, 2 * FOX_HEADS, axis=1))).astype(BF16)

    first = lax.broadcasted_iota(jnp.int32, (tm // 2, LANES), 1) < RET_HEAD_DIM // 2
    pos = pos_ref[0]
    ang = jnp.where(first, pos[:tm // 2], pos[tm // 2:]).astype(F32) * invf_ref[...]
    cos_h, sin_h = jnp.cos(ang), jnp.sin(ang)
    cos_r, sin_r = pltpu.roll(cos_h, RET_HEAD_DIM // 2, axis=1), pltpu.roll(sin_h, RET_HEAD_DIM // 2, axis=1)
    cos = jnp.concatenate([jnp.where(first, cos_h, cos_r), jnp.where(first, cos_r, cos_h)], axis=0)
    sin = jnp.concatenate([jnp.where(first, sin_h, sin_r), jnp.where(first, sin_r, sin_h)], axis=0)
    sin = sin * sgn_ref[...]
    for part, scale in ((0, ret_scale), (1, 1.0)):
        for hd in range(RET_HEADS):
            lo = part * RET_WIDTH + hd * RET_HEAD_DIM
            xh = pr[:, qk0 + lo:qk0 + lo + RET_HEAD_DIM]
            rot = xh * cos + pltpu.roll(xh, RET_HEAD_DIM // 2, axis=1) * sin
            if scale != 1.0:
                rot = rot * scale
            r_ref[0, :, lo:lo + RET_HEAD_DIM] = rot.astype(BF16)
    r_ref[0, :, 2 * RET_WIDTH:3 * RET_WIDTH] = pr[:, qk0 + 2 * RET_WIDTH:qk0 + 3 * RET_WIDTH].astype(BF16)
    rg = pr[:, qk0 + 3 * RET_WIDTH:]
    r_ref[0, :, 3 * RET_WIDTH:] = (rg * jax.nn.sigmoid(rg)).astype(BF16)

    fk = jnp.dot(u, wfk_ref[...], preferred_element_type=F32).astype(BF16)
    qvt = lax.dot_general(wqvt_ref[...], u, _NT, preferred_element_type=F32)
    fqvt_ref[0, :FOX_WIDTH, :] = (qvt[:FOX_WIDTH] * fox_scale).astype(BF16)
    fqvt_ref[0, FOX_WIDTH:, :] = qvt[FOX_WIDTH:].astype(BF16)

    placed = jnp.dot(pieces, place_ref[...], preferred_element_type=F32).astype(BF16)
    for pr_i in range(FOX_WIDTH // LANES):
        fk_ref[0, :, 2 * pr_i * LANES:(2 * pr_i + 1) * LANES] = fk[:, pr_i * LANES:(pr_i + 1) * LANES]
        fk_ref[0, :, (2 * pr_i + 1) * LANES:(2 * pr_i + 2) * LANES] = placed[:, pr_i * LANES:(pr_i + 1) * LANES]


def _forget_placement():
    place = np.zeros((LANES, FOX_WIDTH), np.float32)
    for head in range(FOX_HEADS):
        for piece in range(3):
            place[piece * FOX_HEADS + head, (head // 2) * LANES + 3 * (head % 2) + piece] = 1.0
    return jnp.asarray(place, dtype=BF16)


def _mix_in(h, g, pos, invf, sgn, wr, wfk, wqvt, bfg, wm, bm, tm):
    b, s, d = h.shape
    kern = functools.partial(_mix_in_kernel, ret_scale=RET_HEAD_DIM ** -0.5,
                             fox_scale=FOX_HEAD_DIM ** -0.5 * LOG2E)
    place = _forget_placement()
    tile = lambda w: pl.BlockSpec((1, tm, w), lambda i, j: (i, j, 0))
    return pl.pallas_call(
        kern,
        grid=(b, s // tm),
        in_specs=[tile(d), _const_spec(g.shape), tile(1), _const_spec(invf.shape), _const_spec(sgn.shape),
                  _const_spec(wr.shape),
                  _const_spec(wfk.shape), _const_spec(wqvt.shape), _const_spec(bfg.shape),
                  _const_spec(place.shape), _const_spec(wm.shape), _const_spec(bm.shape)],
        out_specs=[tile(4 * RET_WIDTH), tile(2 * FOX_WIDTH),
                   pl.BlockSpec((1, 2 * FOX_WIDTH, tm), lambda i, j: (i, 0, j)), tile(2 * d)],
        out_shape=[jax.ShapeDtypeStruct((b, s, 4 * RET_WIDTH), BF16),
                   jax.ShapeDtypeStruct((b, s, 2 * FOX_WIDTH), BF16),
                   jax.ShapeDtypeStruct((b, 2 * FOX_WIDTH, s), BF16),
                   jax.ShapeDtypeStruct((b, s, 2 * d), BF16)],
        scratch_shapes=[pltpu.VMEM((8, LANES), F32)],
        compiler_params=_params(2, VMEM_MB["mix_in"]),
        name="mix_in",
    )(h, g, pos, invf, sgn, wr, wfk, wqvt, bfg, place, wm, bm)


def _retention_kernel(q_ref, k_ref, v_ref, g_ref, dmask_ref, qdec_ref, kdec_ref, o_ref, state_ref, *, chunk_decay):
    tr = q_ref.shape[1]

    @pl.when(pl.program_id(1) == 0)
    def _():
        state_ref[...] = jnp.zeros_like(state_ref)

    def tile(ref, c, hd):
        return ref[0, c * RET_CHUNK:(c + 1) * RET_CHUNK, hd * RET_HEAD_DIM:(hd + 1) * RET_HEAD_DIM]

    blocks = [(c, hd) for c in range(tr // RET_CHUNK) for hd in range(RET_HEADS)]
    qk, kv = {}, {}
    for c, hd in blocks:
        k = tile(k_ref, c, hd)
        qk[c, hd] = lax.dot_general(tile(q_ref, c, hd), k, _NT, preferred_element_type=F32)
        k_dec = (k.astype(F32) * kdec_ref[hd]).astype(BF16)
        kv[c, hd] = lax.dot_general(k_dec, tile(v_ref, c, hd), _TN, preferred_element_type=F32)
    states = [state_ref[hd] for hd in range(RET_HEADS)]
    for c, hd in blocks:
        q = tile(q_ref, c, hd)
        scores = (qk[c, hd] * dmask_ref[hd]).astype(BF16)
        q_dec = (q.astype(F32) * qdec_ref[hd]).astype(BF16)
        y = jnp.dot(jnp.concatenate([scores, q_dec], axis=1),
                    jnp.concatenate([tile(v_ref, c, hd), states[hd].astype(BF16)], axis=0),
                    preferred_element_type=F32)
        states[hd] = chunk_decay[hd] * states[hd] + kv[c, hd]
        mu = jnp.mean(y, axis=-1, keepdims=True)
        yc = y - mu
        var = jnp.mean(yc * yc, axis=-1, keepdims=True)
        yn = yc * lax.rsqrt(var + EPS)
        rows = slice(c * RET_CHUNK, (c + 1) * RET_CHUNK)
        cols = slice(hd * RET_HEAD_DIM, (hd + 1) * RET_HEAD_DIM)
        o_ref[0, rows, cols] = (yn * g_ref[0, rows, cols].astype(F32)).astype(BF16)
    for hd in range(RET_HEADS):
        state_ref[hd] = states[hd]


def _retention(r, dmask, qdec, kdec, chunk_decay, tr):
    b, s, _ = r.shape
    kern = functools.partial(_retention_kernel, chunk_decay=chunk_decay)
    part = lambda p: pl.BlockSpec((1, tr, RET_WIDTH), lambda i, j: (i, j, p))
    return pl.pallas_call(
        kern,
        grid=(b, s // tr),
        in_specs=[part(0), part(1), part(2), part(3),
                  _const_spec(dmask.shape), _const_spec(qdec.shape), _const_spec(kdec.shape)],
        out_specs=pl.BlockSpec((1, tr, RET_WIDTH), lambda i, j: (i, j, 0)),
        out_shape=jax.ShapeDtypeStruct((b, s, RET_WIDTH), BF16),
        scratch_shapes=[pltpu.VMEM((RET_HEADS, RET_HEAD_DIM, RET_HEAD_DIM), F32)],
        compiler_params=_params(2, VMEM_MB["retention"]),
        name="retention",
    )(r, r, r, r, dmask, qdec, kdec)


def _fox_schedule(nq):
    remaining = {qi: list(range(qi)) for qi in range(1, nq)}
    rounds = []
    while any(remaining.values()):
        pick = sorted((q for q in remaining if remaining[q]), key=lambda q: (-len(remaining[q]), q))[:FOX_ROUND]
        rounds.append([(q, remaining[q].pop(), q) for q in pick])
    n_plain = len(rounds)
    diag = [(q, q, q) for q in range(nq)]
    rounds += [diag[i:i + FOX_ROUND] for i in range(0, nq, FOX_ROUND)]
    flat = [t for rnd in rounds for t in rnd + [(0, 0, nq)] * (FOX_ROUND - len(rnd))]
    tables = tuple(jnp.asarray([t[i] for t in flat], dtype=jnp.int32) for i in range(3))
    return tables, n_plain, len(rounds)


def _fox_kernel(qsrc_ref, ksrc_ref, slot_ref, qt_ref, k_ref, vt_ref, o_ref, m_ref, acc_ref, *,
                tq, n_plain, n_rounds):
    s_len = k_ref.shape[1]
    nq = s_len // tq

    m_ref[...] = jnp.full(m_ref.shape, NEG_BIG, F32)
    acc_ref[...] = jnp.zeros(acc_ref.shape, F32)

    key_idx = lax.broadcasted_iota(jnp.int32, (tq, tq), 0)
    qry_idx = lax.broadcasted_iota(jnp.int32, (tq, tq), 1)
    causal = key_idx <= qry_idx
    half_zero = jnp.zeros((FOX_HEAD_DIM, tq), BF16)
    ones_rows = jnp.ones((FOX_ACC_ROWS - FOX_HEAD_DIM, tq), BF16)
    piece_row = lax.broadcasted_iota(jnp.int32, (LANES, tq), 0)
    pick_c = [jnp.where((piece_row >= 3 * hd) & (piece_row < 3 * hd + 3), -1.0, 0.0).astype(BF16) for hd in range(2)]
    chains = [(t, hd) for t in range(FOX_ROUND) for hd in range(2)]

    def run_round(rnd, masked):
        slots = [slot_ref[rnd * FOX_ROUND + t] for t in range(FOX_ROUND)]
        k0s = [pl.multiple_of(ksrc_ref[rnd * FOX_ROUND + t] * tq, tq) for t in range(FOX_ROUND)]
        qk = {}

        def scores(t):
            q0 = pl.multiple_of(qsrc_ref[rnd * FOX_ROUND + t] * tq, tq)
            qt = qt_ref[0, :, pl.ds(q0, tq)]
            qts = (jnp.concatenate([qt[:FOX_HEAD_DIM], half_zero, pick_c[0]], axis=0),
                   jnp.concatenate([half_zero, qt[FOX_HEAD_DIM:], pick_c[1]], axis=0))
            k2 = k_ref[0, pl.ds(k0s[t], tq), :]
            for hd in range(2):
                qk[t, hd] = jnp.dot(k2, qts[hd], preferred_element_type=F32)

        for t in range(FOX_LOOKAHEAD):
            scores(t)
        for t, hd in chains:
            if hd == 0 and t + FOX_LOOKAHEAD < FOX_ROUND:
                scores(t + FOX_LOOKAHEAD)
            s = qk[t, hd]
            if masked:
                s = jnp.where(causal, s, NEG_BIG)
            m_old = m_ref[slots[t], hd]
            m_new = jnp.maximum(m_old, jnp.max(s, axis=0, keepdims=True))
            m_ref[slots[t], hd] = m_new
            p = jnp.exp2(s - m_new).astype(BF16)
            vt = vt_ref[0, hd * FOX_HEAD_DIM:(hd + 1) * FOX_HEAD_DIM, pl.ds(k0s[t], tq)]
            vt = jnp.concatenate([vt, ones_rows], axis=0)
            acc_ref[slots[t], hd] = (jnp.exp2(m_old - m_new) * acc_ref[slots[t], hd]
                                     + jnp.dot(vt, p, preferred_element_type=F32))
        return 0

    def sweep(lo, hi, masked):
        def rounds(i, _):
            for j in range(FOX_UNROLL):
                run_round(lo + FOX_UNROLL * i + j, masked)
            return 0
        lax.fori_loop(0, (hi - lo) // FOX_UNROLL, rounds, 0)
        for rnd in range(hi - (hi - lo) % FOX_UNROLL, hi):
            run_round(rnd, masked)

    sweep(0, n_plain, False)
    sweep(n_plain, n_rounds, True)

    def finish(qi, _):
        outs = []
        for hd in range(2):
            acc = acc_ref[qi, hd]
            outs.append(acc[:FOX_HEAD_DIM] / acc[FOX_HEAD_DIM:FOX_HEAD_DIM + 1])
        q0 = pl.multiple_of(qi * tq, tq)
        o_ref[0, :, pl.ds(q0, tq)] = jnp.concatenate(outs, axis=0).astype(BF16)
        return 0

    lax.fori_loop(0, nq, finish, 0)


def _fox(fk, fqvt, tq):
    b, s, _ = fk.shape
    nq = s // tq
    pairs = FOX_WIDTH // LANES
    tables, n_plain, n_rounds = _fox_schedule(nq)
    kern = functools.partial(_fox_kernel, tq=tq, n_plain=n_plain, n_rounds=n_rounds)
    grid_spec = pltpu.PrefetchScalarGridSpec(
        num_scalar_prefetch=len(tables),
        grid=(b, pairs),
        in_specs=[pl.BlockSpec((1, LANES, s), lambda i, j, *_: (i, j, 0)),
                  pl.BlockSpec((1, s, 2 * LANES), lambda i, j, *_: (i, 0, j)),
                  pl.BlockSpec((1, LANES, s), lambda i, j, *_: (i, pairs + j, 0))],
        out_specs=pl.BlockSpec((1, LANES, s), lambda i, j, *_: (i, j, 0)),
        scratch_shapes=[pltpu.VMEM((nq + 1, 2, 1, tq), F32),
                        pltpu.VMEM((nq + 1, 2, FOX_ACC_ROWS, tq), F32)],
    )
    return pl.pallas_call(
        kern,
        grid_spec=grid_spec,
        out_shape=jax.ShapeDtypeStruct((b, FOX_WIDTH, s), BF16),
        compiler_params=_params(2, VMEM_MB["fox"]),
        name="fox",
    )(*tables, fqvt, fk, fqvt)


def _post_kernel(h_ref, yr_ref, yf_ref, gate_ref, p_ref, wro_ref, wfo_ref, wo_ref, g2_ref, wg_ref, wu_ref, wd_ref,
                 gp_ref, wpg_ref, wp_ref, gf_ref, o_ref):
    d = h_ref.shape[1]
    z_a = jnp.dot(yr_ref[...], wro_ref[...], preferred_element_type=F32)
    z_b = lax.dot_general(yf_ref[0], wfo_ref[...], _TN, preferred_element_type=F32)
    mixed = gate_ref[:, :d].astype(F32) * z_a + gate_ref[:, d:].astype(F32) * z_b
    h = h_ref[...] + jnp.dot(mixed.astype(BF16), wo_ref[...], preferred_element_type=F32)
    emb = jnp.dot(p_ref[...].astype(BF16), wp_ref[...], preferred_element_type=F32)
    h = h + 0.5 * _swiglu_tile(_rms_norm(h, g2_ref[...]).astype(BF16), wg_ref, wu_ref, wd_ref)
    hn = _rms_norm(h, gp_ref[...]).astype(BF16)
    half = h.shape[0] // 2
    for rows in (slice(0, half), slice(half, None)):
        ple_gate = jax.nn.sigmoid(jnp.dot(hn[rows], wpg_ref[...], preferred_element_type=F32))
        o_ref[rows, :] = _rms_norm(h[rows] + ple_gate * emb[rows], gf_ref[...])


def _post(h, yr, yf_t, gates, p, wro, wfo, wo, g2, wg, wu, wd, gp, wpg, wp, gf, tm):
    t, d = h.shape
    per_seq = yf_t.shape[2] // tm
    tile = lambda w: pl.BlockSpec((tm, w), lambda i: (i, 0))
    consts = [wro, wfo, wo, g2, wg, wu, wd, gp, wpg, wp, gf]
    return pl.pallas_call(
        _post_kernel,
        grid=(t // tm,),
        in_specs=[tile(d), tile(RET_WIDTH),
                  pl.BlockSpec((1, FOX_WIDTH, tm), lambda i: (i // per_seq, 0, i % per_seq)),
                  tile(2 * d), tile(p.shape[1])]
        + [_const_spec(c.shape) for c in consts],
        out_specs=tile(d),
        out_shape=jax.ShapeDtypeStruct((t, d), F32),
        compiler_params=_params(1, VMEM_MB["post"]),
        name="post",
    )(h, yr, yf_t, gates, p, *consts)


def _retention_tables():
    hd = np.arange(RET_HEADS, dtype=np.float64)
    log_gamma = np.log1p(-np.exp2(-5.0 - hd))
    idx = np.arange(RET_CHUNK, dtype=np.float64)
    diff = idx[:, None] - idx[None, :]
    dmask = np.where(diff >= 0, np.exp(log_gamma[:, None, None] * np.maximum(diff, 0.0)), 0.0)
    ones = np.ones((1, 1, RET_HEAD_DIM))
    qdec = np.exp(log_gamma[:, None] * (idx + 1.0))[:, :, None] * ones
    kdec = np.exp(log_gamma[:, None] * (RET_CHUNK - 1 - idx))[:, :, None] * ones
    chunk_decay = tuple(float(x) for x in np.exp(log_gamma * RET_CHUNK))
    as_f32 = lambda a: jnp.asarray(a, dtype=F32)
    return as_f32(dmask), as_f32(qdec), as_f32(kdec), chunk_decay


def _layer(h, p, positions, ln_ffn1, w_ffn1_gate, w_ffn1_up, w_ffn1_down, ln_mix, w_in, b_forget, w_merge,
           b_merge, w_ret_out, w_fox_out, w_out, ln_ffn2, w_ffn2_gate, w_ffn2_up, w_ffn2_down, ln_ple, w_ple,
           w_ple_gate, ln_final):
    b, s, d = h.shape
    t = b * s
    tm = min(TOKEN_TILE, s)
    row = lambda v: v.reshape(1, -1).astype(F32)

    inv_freq = 1.0 / (ROPE_BASE ** (jnp.arange(RET_HEAD_DIM // 2, dtype=F32) / (RET_HEAD_DIM // 2)))
    invf = jnp.concatenate([inv_freq, inv_freq]).reshape(1, RET_HEAD_DIM)
    sgn = jnp.concatenate([-jnp.ones((RET_HEAD_DIM // 2,), F32), jnp.ones((RET_HEAD_DIM // 2,), F32)]).reshape(1, -1)
    h1 = _ffn(h.reshape(t, d), row(ln_ffn1), w_ffn1_gate.astype(BF16), w_ffn1_up.astype(BF16),
              w_ffn1_down.astype(BF16), tm)

    even_odd = lambda w: w.reshape(d, RET_HEADS, RET_HEAD_DIM // 2, 2).transpose(0, 1, 3, 2).reshape(d, RET_WIDTH)
    w_rq = even_odd(w_in[:, :RET_WIDTH].astype(BF16))
    w_rk = even_odd(w_in[:, RET_WIDTH:2 * RET_WIDTH].astype(BF16))
    fox_lo = 4 * RET_WIDTH
    fox_hi = fox_lo + 3 * FOX_WIDTH
    w_ff = jnp.pad(w_in[:, fox_hi:], ((0, 0), (0, LANES - FOX_HEADS)))
    wr = jnp.concatenate([w_ff, w_rq, w_rk, w_in[:, 2 * RET_WIDTH:fox_lo]], axis=1).astype(BF16)
    w_fq, w_fk, w_fv = (w_in[:, fox_lo + i * FOX_WIDTH:fox_lo + (i + 1) * FOX_WIDTH] for i in range(3))
    wqvt = jnp.concatenate([w_fq, w_fv], axis=1).T.astype(BF16)
    bfg = jnp.pad(b_forget.astype(F32), (0, LANES - FOX_HEADS)).reshape(1, LANES)
    r, fk, fqvt, gates = _mix_in(h1.reshape(b, s, d), row(ln_mix), positions.reshape(b, s, 1), invf, sgn, wr,
                                 w_fk.astype(BF16), wqvt, bfg, w_merge.astype(BF16), row(b_merge), tm)

    dmask, qdec, kdec, chunk_decay = _retention_tables()
    y_ret = _retention(r, dmask, qdec, kdec, chunk_decay, min(RET_TILE, s))
    y_fox = _fox(fk, fqvt, min(FOX_TILE, s))

    assert ln_final is not None, "multi-layer stacks need a post kernel without the final norm"
    out = _post(h1, y_ret.reshape(t, RET_WIDTH), y_fox, gates.reshape(t, 2 * d),
                p.reshape(t, -1), w_ret_out.astype(BF16), w_fox_out.astype(BF16), w_out.astype(BF16), row(ln_ffn2),
                w_ffn2_gate.astype(BF16), w_ffn2_up.astype(BF16), w_ffn2_down.astype(BF16), row(ln_ple),
                w_ple_gate.astype(BF16), w_ple.astype(BF16), row(ln_final), tm)
    return out.reshape(b, s, d)


def kernel(x, p, positions, ln_ffn1, w_ffn1_gate, w_ffn1_up, w_ffn1_down, ln_mix, w_in, b_forget, w_merge, b_merge,
           w_ret_out, w_fox_out, w_out, ln_ffn2, w_ffn2_gate, w_ffn2_up, w_ffn2_down, ln_ple, w_ple, w_ple_gate,
           ln_final):
    depth = p.shape[0]
    assert depth == 1, "this implementation covers the single-layer configuration"
    return _layer(x, p[0], positions, ln_ffn1[0], w_ffn1_gate[0], w_ffn1_up[0], w_ffn1_down[0], ln_mix[0], w_in[0],
                  b_forget[0], w_merge[0], b_merge[0], w_ret_out[0], w_fox_out[0], w_out[0], ln_ffn2[0],
                  w_ffn2_gate[0], w_ffn2_up[0], w_ffn2_down[0], ln_ple[0], w_ple[0], w_ple_gate[0], ln_final)
```

```python
import functools

import numpy as np
import jax
import jax.numpy as jnp
from jax import lax
from jax.experimental import pallas as pl
from jax.experimental.pallas import tpu as pltpu

F32 = jnp.float32
BF16 = jnp.bfloat16

EPS = 1e-6
ROPE_BASE = 10000.0
RET_HEADS = 4
RET_HEAD_DIM = 128
RET_WIDTH = RET_HEADS * RET_HEAD_DIM
FOX_HEADS = 8
FOX_HEAD_DIM = 64
FOX_WIDTH = FOX_HEADS * FOX_HEAD_DIM
RET_CHUNK = 128

LANES = 128
FF_CHUNK = 256
NEG_BIG = -1e30

TOKEN_TILE = 512
RET_TILE = 2048
RET_GROUP = 4
FOX_TILE = 256
VMEM_MB = {"ffn": 48, "mix_in": 48, "retention": 32, "fox": 40, "post": 58}
LOG2E = 1.4426950408889634
FOX_ROUND = 8
FOX_UNROLL = 5
FOX_LOOKAHEAD = 2
FOX_ACC_ROWS = FOX_HEAD_DIM + 16

_NT = (((1,), (1,)), ((), ()))
_TN = (((0,), (0,)), ((), ()))


def _const_spec(shape):
    zeros = (0,) * len(shape)
    return pl.BlockSpec(shape, lambda *_: zeros, pipeline_mode=pl.Buffered(1))


def _params(n_axes, vmem_mb):
    return pltpu.CompilerParams(
        dimension_semantics=("arbitrary",) * n_axes,
        vmem_limit_bytes=vmem_mb * 1024 * 1024,
    )


def _rms_norm(x, g):
    return x * lax.rsqrt(jnp.mean(x * x, axis=-1, keepdims=True) + EPS) * g


def _swiglu_tile(xn, wg_ref, wu_ref, wd_ref):
    acc = None
    for c in range(wg_ref.shape[1] // FF_CHUNK):
        cols = slice(c * FF_CHUNK, (c + 1) * FF_CHUNK)
        g = jnp.dot(xn, wg_ref[:, cols], preferred_element_type=F32)
        u = jnp.dot(xn, wu_ref[:, cols], preferred_element_type=F32)
        a = (g * jax.nn.sigmoid(g) * u).astype(BF16)
        d = jnp.dot(a, wd_ref[cols, :], preferred_element_type=F32)
        acc = d if acc is None else acc + d
    return acc


def _ffn_kernel(x_ref, g_ref, wg_ref, wu_ref, wd_ref, o_ref):
    x = x_ref[...]
    xn = _rms_norm(x, g_ref[...]).astype(BF16)
    o_ref[...] = x + 0.5 * _swiglu_tile(xn, wg_ref, wu_ref, wd_ref)


def _ffn(x, g, wg, wu, wd, tm):
    t, d = x.shape
    tile = pl.BlockSpec((tm, d), lambda i: (i, 0))
    return pl.pallas_call(
        _ffn_kernel,
        grid=(t // tm,),
        in_specs=[tile, _const_spec(g.shape), _const_spec(wg.shape), _const_spec(wu.shape), _const_spec(wd.shape)],
        out_specs=tile,
        out_shape=jax.ShapeDtypeStruct((t, d), F32),
        compiler_params=_params(1, VMEM_MB["ffn"]),
        name="ffn",
    )(x, g, wg, wu, wd)


def _mix_in_kernel(h_ref, g_ref, pos_ref, invf_ref, sgn_ref, wr_ref, wfk_ref, wqvt_ref, bf_ref,
                   place_ref, wm_ref, bm_ref, r_ref, fk_ref, fqvt_ref, gate_ref, carry_ref, *,
                   ret_scale, fox_scale):
    tm = h_ref.shape[1]
    u = _rms_norm(h_ref[0], g_ref[...]).astype(BF16)

    gm = jnp.dot(u, wm_ref[...], preferred_element_type=F32) + bm_ref[...]
    gate_ref[0] = jax.nn.sigmoid(gm).astype(BF16)

    pr = jnp.dot(u, wr_ref[...], preferred_element_type=F32)
    qk0 = LANES
    ff = pr[:, :LANES] + bf_ref[...]
    logf = jnp.minimum(ff, 0.0) - jnp.log1p(jnp.exp(-jnp.abs(ff)))
    tok = lax.broadcasted_iota(jnp.int32, logf.shape, 0)
    cum = logf
    sh = 1
    while sh < tm:
        cum = cum + jnp.where(tok >= sh, pltpu.roll(cum, sh, axis=0), 0.0)
        sh *= 2

    @pl.when(pl.program_id(1) == 0)
    def _():
        carry_ref[...] = jnp.zeros_like(carry_ref)

    cum = cum + carry_ref[0:1, :]
    carry_ref[...] = jnp.broadcast_to(cum[tm - 1:tm, :], carry_ref.shape)

    c2 = cum * LOG2E
    c_hi = c2.astype(BF16).astype(F32)
    rest = c2 - c_hi
    c_mid = rest.astype(BF16).astype(F32)
    c_lo = rest - c_mid
    lane = lax.broadcasted_iota(jnp.int32, c2.shape, 1)
    pieces = jnp.where(lane < FOX_HEADS, c_hi,
                       jnp.where(lane < 2 * FOX_HEADS, pltpu.roll(c_mid, FOX_HEADS, axis=1),
                                 pltpu.roll(c_lo, 2 * FOX_HEADS, axis=1))).astype(BF16)

    first = lax.broadcasted_iota(jnp.int32, (tm // 2, LANES), 1) < RET_HEAD_DIM // 2
    pos = pos_ref[0]
    ang = jnp.where(first, pos[:tm // 2], pos[tm // 2:]).astype(F32) * invf_ref[...]
    cos_h, sin_h = jnp.cos(ang), jnp.sin(ang)
    cos_r, sin_r = pltpu.roll(cos_h, RET_HEAD_DIM // 2, axis=1), pltpu.roll(sin_h, RET_HEAD_DIM // 2, axis=1)
    cos = jnp.concatenate([jnp.where(first, cos_h, cos_r), jnp.where(first, cos_r, cos_h)], axis=0)
    sin = jnp.concatenate([jnp.where(first, sin_h, sin_r), jnp.where(first, sin_r, sin_h)], axis=0)
    sin = sin * sgn_ref[...]
    for part, scale in ((0, ret_scale), (1, 1.0)):
        for hd in range(RET_HEADS):
            lo = part * RET_WIDTH + hd * RET_HEAD_DIM
            xh = pr[:, qk0 + lo:qk0 + lo + RET_HEAD_DIM]
            rot = xh * cos + pltpu.roll(xh, RET_HEAD_DIM // 2, axis=1) * sin
            if scale != 1.0:
                rot = rot * scale
            r_ref[0, :, lo:lo + RET_HEAD_DIM] = rot.astype(BF16)
    r_ref[0, :, 2 * RET_WIDTH:3 * RET_WIDTH] = pr[:, qk0 + 2 * RET_WIDTH:qk0 + 3 * RET_WIDTH].astype(BF16)
    rg = pr[:, qk0 + 3 * RET_WIDTH:]
    r_ref[0, :, 3 * RET_WIDTH:] = (rg * jax.nn.sigmoid(rg)).astype(BF16)

    fk = jnp.dot(u, wfk_ref[...], preferred_element_type=F32).astype(BF16)
    qvt = lax.dot_general(wqvt_ref[...], u, _NT, preferred_element_type=F32)
    fqvt_ref[0, :FOX_WIDTH, :] = (qvt[:FOX_WIDTH] * fox_scale).astype(BF16)
    fqvt_ref[0, FOX_WIDTH:, :] = qvt[FOX_WIDTH:].astype(BF16)

    placed = jnp.dot(pieces, place_ref[...], preferred_element_type=F32).astype(BF16)
    for pr_i in range(FOX_WIDTH // LANES):
        fk_ref[0, :, 2 * pr_i * LANES:(2 * pr_i + 1) * LANES] = fk[:, pr_i * LANES:(pr_i + 1) * LANES]
        fk_ref[0, :, (2 * pr_i + 1) * LANES:(2 * pr_i + 2) * LANES] = placed[:, pr_i * LANES:(pr_i + 1) * LANES]


def _forget_placement():
    place = np.zeros((LANES, FOX_WIDTH), np.float32)
    for head in range(FOX_HEADS):
        for piece in range(3):
            place[piece * FOX_HEADS + head, (head // 2) * LANES + 3 * (head % 2) + piece] = 1.0
    return jnp.asarray(place, dtype=BF16)


def _mix_in(h, g, pos, invf, sgn, wr, wfk, wqvt, bfg, wm, bm, tm):
    b, s, d = h.shape
    kern = functools.partial(_mix_in_kernel, ret_scale=RET_HEAD_DIM ** -0.5,
                             fox_scale=FOX_HEAD_DIM ** -0.5 * LOG2E)
    place = _forget_placement()
    tile = lambda w: pl.BlockSpec((1, tm, w), lambda i, j: (i, j, 0))
    return pl.pallas_call(
        kern,
        grid=(b, s // tm),
        in_specs=[tile(d), _const_spec(g.shape), tile(1), _const_spec(invf.shape), _const_spec(sgn.shape),
                  _const_spec(wr.shape),
                  _const_spec(wfk.shape), _const_spec(wqvt.shape), _const_spec(bfg.shape),
                  _const_spec(place.shape), _const_spec(wm.shape), _const_spec(bm.shape)],
        out_specs=[tile(4 * RET_WIDTH), tile(2 * FOX_WIDTH),
                   pl.BlockSpec((1, 2 * FOX_WIDTH, tm), lambda i, j: (i, 0, j)), tile(2 * d)],
        out_shape=[jax.ShapeDtypeStruct((b, s, 4 * RET_WIDTH), BF16),
                   jax.ShapeDtypeStruct((b, s, 2 * FOX_WIDTH), BF16),
                   jax.ShapeDtypeStruct((b, 2 * FOX_WIDTH, s), BF16),
                   jax.ShapeDtypeStruct((b, s, 2 * d), BF16)],
        scratch_shapes=[pltpu.VMEM((8, LANES), F32)],
        compiler_params=_params(2, VMEM_MB["mix_in"]),
        name="mix_in",
    )(h, g, pos, invf, sgn, wr, wfk, wqvt, bfg, place, wm, bm)


def _retention_kernel(q_ref, k_ref, v_ref, g_ref, dmask_ref, qdec_ref, kdec_ref, o_ref, state_ref, *, chunk_decay):
    tr = q_ref.shape[1]

    @pl.when(pl.program_id(1) == 0)
    def _():
        state_ref[...] = jnp.zeros_like(state_ref)

    def tile(ref, c, hd):
        return ref[0, c * RET_CHUNK:(c + 1) * RET_CHUNK, hd * RET_HEAD_DIM:(hd + 1) * RET_HEAD_DIM]

    n_chunks = tr // RET_CHUNK
    groups = [range(g, min(g + RET_GROUP, n_chunks)) for g in range(0, n_chunks, RET_GROUP)]
    qk, kv = {}, {}

    def issue(chunks):
        for c in chunks:
            for hd in range(RET_HEADS):
                k = tile(k_ref, c, hd)
                qk[c, hd] = lax.dot_general(tile(q_ref, c, hd), k, _NT, preferred_element_type=F32)
                k_dec = (k.astype(F32) * kdec_ref[hd]).astype(BF16)
                kv[c, hd] = lax.dot_general(k_dec, tile(v_ref, c, hd), _TN, preferred_element_type=F32)

    states = [state_ref[hd] for hd in range(RET_HEADS)]
    issue(groups[0])
    for gi, chunks in enumerate(groups):
        if gi + 1 < len(groups):
            issue(groups[gi + 1])
        for c in chunks:
            for hd in range(RET_HEADS):
                q = tile(q_ref, c, hd)
                scores = (qk.pop((c, hd)) * dmask_ref[hd]).astype(BF16)
                q_dec = (q.astype(F32) * qdec_ref[hd]).astype(BF16)
                y = jnp.dot(jnp.concatenate([scores, q_dec], axis=1),
                            jnp.concatenate([tile(v_ref, c, hd), states[hd].astype(BF16)], axis=0),
                            preferred_element_type=F32)
                states[hd] = chunk_decay[hd] * states[hd] + kv.pop((c, hd))
                mu = jnp.mean(y, axis=-1, keepdims=True)
                yc = y - mu
                var = jnp.mean(yc * yc, axis=-1, keepdims=True)
                yn = yc * lax.rsqrt(var + EPS)
                rows = slice(c * RET_CHUNK, (c + 1) * RET_CHUNK)
                cols = slice(hd * RET_HEAD_DIM, (hd + 1) * RET_HEAD_DIM)
                o_ref[0, rows, cols] = (yn * g_ref[0, rows, cols].astype(F32)).astype(BF16)
    for hd in range(RET_HEADS):
        state_ref[hd] = states[hd]


def _retention(r, dmask, qdec, kdec, chunk_decay, tr):
    b, s, _ = r.shape
    kern = functools.partial(_retention_kernel, chunk_decay=chunk_decay)
    part = lambda p: pl.BlockSpec((1, tr, RET_WIDTH), lambda i, j: (i, j, p))
    return pl.pallas_call(
        kern,
        grid=(b, s // tr),
        in_specs=[part(0), part(1), part(2), part(3),
                  _const_spec(dmask.shape), _const_spec(qdec.shape), _const_spec(kdec.shape)],
        out_specs=pl.BlockSpec((1, tr, RET_WIDTH), lambda i, j: (i, j, 0)),
        out_shape=jax.ShapeDtypeStruct((b, s, RET_WIDTH), BF16),
        scratch_shapes=[pltpu.VMEM((RET_HEADS, RET_HEAD_DIM, RET_HEAD_DIM), F32)],
        compiler_params=_params(2, VMEM_MB["retention"]),
        name="retention",
    )(r, r, r, r, dmask, qdec, kdec)


def _fox_schedule(nq):
    remaining = {qi: list(range(qi)) for qi in range(1, nq)}
    rounds = []
    while any(remaining.values()):
        pick = sorted((q for q in remaining if remaining[q]), key=lambda q: (-len(remaining[q]), q))[:FOX_ROUND]
        rounds.append([(q, remaining[q].pop(), q) for q in pick])
    n_plain = len(rounds)
    diag = [(q, q, q) for q in range(nq)]
    rounds += [diag[i:i + FOX_ROUND] for i in range(0, nq, FOX_ROUND)]
    flat = [t for rnd in rounds for t in rnd + [(0, 0, nq)] * (FOX_ROUND - len(rnd))]
    tables = tuple(jnp.asarray([t[i] for t in flat], dtype=jnp.int32) for i in range(3))
    return tables, n_plain, len(rounds)


def _fox_kernel(qsrc_ref, ksrc_ref, slot_ref, qt_ref, k_ref, vt_ref, o_ref, m_ref, acc_ref, *,
                tq, n_plain, n_rounds):
    s_len = k_ref.shape[1]
    nq = s_len // tq

    m_ref[...] = jnp.full(m_ref.shape, NEG_BIG, F32)
    acc_ref[...] = jnp.zeros(acc_ref.shape, F32)

    key_idx = lax.broadcasted_iota(jnp.int32, (tq, tq), 0)
    qry_idx = lax.broadcasted_iota(jnp.int32, (tq, tq), 1)
    causal = key_idx <= qry_idx
    half_zero = jnp.zeros((FOX_HEAD_DIM, tq), BF16)
    ones_rows = jnp.ones((FOX_ACC_ROWS - FOX_HEAD_DIM, tq), BF16)
    piece_row = lax.broadcasted_iota(jnp.int32, (LANES, tq), 0)
    pick_c = [jnp.where((piece_row >= 3 * hd) & (piece_row < 3 * hd + 3), -1.0, 0.0).astype(BF16) for hd in range(2)]
    chains = [(t, hd) for t in range(FOX_ROUND) for hd in range(2)]

    def run_round(rnd, masked):
        slots = [slot_ref[rnd * FOX_ROUND + t] for t in range(FOX_ROUND)]
        k0s = [pl.multiple_of(ksrc_ref[rnd * FOX_ROUND + t] * tq, tq) for t in range(FOX_ROUND)]
        qk = {}

        def scores(t):
            q0 = pl.multiple_of(qsrc_ref[rnd * FOX_ROUND + t] * tq, tq)
            qt = qt_ref[0, :, pl.ds(q0, tq)]
            qts = (jnp.concatenate([qt[:FOX_HEAD_DIM], half_zero, pick_c[0]], axis=0),
                   jnp.concatenate([half_zero, qt[FOX_HEAD_DIM:], pick_c[1]], axis=0))
            k2 = k_ref[0, pl.ds(k0s[t], tq), :]
            for hd in range(2):
                qk[t, hd] = jnp.dot(k2, qts[hd], preferred_element_type=F32)

        for t in range(FOX_LOOKAHEAD):
            scores(t)
        for t, hd in chains:
            if hd == 0 and t + FOX_LOOKAHEAD < FOX_ROUND:
                scores(t + FOX_LOOKAHEAD)
            s = qk[t, hd]
            if masked:
                s = jnp.where(causal, s, NEG_BIG)
            m_old = m_ref[slots[t], hd]
            m_new = jnp.maximum(m_old, jnp.max(s, axis=0, keepdims=True))
            m_ref[slots[t], hd] = m_new
            p = jnp.exp2(s - m_new).astype(BF16)
            vt = vt_ref[0, hd * FOX_HEAD_DIM:(hd + 1) * FOX_HEAD_DIM, pl.ds(k0s[t], tq)]
            vt = jnp.concatenate([vt, ones_rows], axis=0)
            acc_ref[slots[t], hd] = (jnp.exp2(m_old - m_new) * acc_ref[slots[t], hd]
                                     + jnp.dot(vt, p, preferred_element_type=F32))
        return 0

    def sweep(lo, hi, masked):
        def rounds(i, _):
            for j in range(FOX_UNROLL):
                run_round(lo + FOX_UNROLL * i + j, masked)
            return 0
        lax.fori_loop(0, (hi - lo) // FOX_UNROLL, rounds, 0)
        for rnd in range(hi - (hi - lo) % FOX_UNROLL, hi):
            run_round(rnd, masked)

    sweep(0, n_plain, False)
    sweep(n_plain, n_rounds, True)

    def finish(qi, _):
        outs = []
        for hd in range(2):
            acc = acc_ref[qi, hd]
            outs.append(acc[:FOX_HEAD_DIM] / acc[FOX_HEAD_DIM:FOX_HEAD_DIM + 1])
        q0 = pl.multiple_of(qi * tq, tq)
        o_ref[0, :, pl.ds(q0, tq)] = jnp.concatenate(outs, axis=0).astype(BF16)
        return 0

    lax.fori_loop(0, nq, finish, 0)


def _fox(fk, fqvt, tq):
    b, s, _ = fk.shape
    nq = s // tq
    pairs = FOX_WIDTH // LANES
    tables, n_plain, n_rounds = _fox_schedule(nq)
    kern = functools.partial(_fox_kernel, tq=tq, n_plain=n_plain, n_rounds=n_rounds)
    grid_spec = pltpu.PrefetchScalarGridSpec(
        num_scalar_prefetch=len(tables),
        grid=(b, pairs),
        in_specs=[pl.BlockSpec((1, LANES, s), lambda i, j, *_: (i, j, 0)),
                  pl.BlockSpec((1, s, 2 * LANES), lambda i, j, *_: (i, 0, j)),
                  pl.BlockSpec((1, LANES, s), lambda i, j, *_: (i, pairs + j, 0))],
        out_specs=pl.BlockSpec((1, LANES, s), lambda i, j, *_: (i, j, 0)),
        scratch_shapes=[pltpu.VMEM((nq + 1, 2, 1, tq), F32),
                        pltpu.VMEM((nq + 1, 2, FOX_ACC_ROWS, tq), F32)],
    )
    return pl.pallas_call(
        kern,
        grid_spec=grid_spec,
        out_shape=jax.ShapeDtypeStruct((b, FOX_WIDTH, s), BF16),
        compiler_params=_params(2, VMEM_MB["fox"]),
        name="fox",
    )(*tables, fqvt, fk, fqvt)


def _post_kernel(h_ref, yr_ref, yf_ref, gate_ref, p_ref, wro_ref, wfo_ref, wo_ref, g2_ref, wg_ref, wu_ref, wd_ref,
                 gp_ref, wpg_ref, wp_ref, gf_ref, o_ref):
    d = h_ref.shape[1]
    z_a = jnp.dot(yr_ref[...], wro_ref[...], preferred_element_type=F32)
    z_b = lax.dot_general(yf_ref[0], wfo_ref[...], _TN, preferred_element_type=F32)
    mixed = gate_ref[:, :d].astype(F32) * z_a + gate_ref[:, d:].astype(F32) * z_b
    h = h_ref[...] + jnp.dot(mixed.astype(BF16), wo_ref[...], preferred_element_type=F32)
    emb = jnp.dot(p_ref[...].astype(BF16), wp_ref[...], preferred_element_type=F32)
    h = h + 0.5 * _swiglu_tile(_rms_norm(h, g2_ref[...]).astype(BF16), wg_ref, wu_ref, wd_ref)
    hn = _rms_norm(h, gp_ref[...]).astype(BF16)
    half = h.shape[0] // 2
    for rows in (slice(0, half), slice(half, None)):
        ple_gate = jax.nn.sigmoid(jnp.dot(hn[rows], wpg_ref[...], preferred_element_type=F32))
        o_ref[rows, :] = _rms_norm(h[rows] + ple_gate * emb[rows], gf_ref[...])


def _post(h, yr, yf_t, gates, p, wro, wfo, wo, g2, wg, wu, wd, gp, wpg, wp, gf, tm):
    t, d = h.shape
    per_seq = yf_t.shape[2] // tm
    tile = lambda w: pl.BlockSpec((tm, w), lambda i: (i, 0))
    consts = [wro, wfo, wo, g2, wg, wu, wd, gp, wpg, wp, gf]
    return pl.pallas_call(
        _post_kernel,
        grid=(t // tm,),
        in_specs=[tile(d), tile(RET_WIDTH),
                  pl.BlockSpec((1, FOX_WIDTH, tm), lambda i: (i // per_seq, 0, i % per_seq)),
                  tile(2 * d), tile(p.shape[1])]
        + [_const_spec(c.shape) for c in consts],
        out_specs=tile(d),
        out_shape=jax.ShapeDtypeStruct((t, d), F32),
        compiler_params=_params(1, VMEM_MB["post"]),
        name="post",
    )(h, yr, yf_t, gates, p, *consts)


def _retention_tables():
    hd = np.arange(RET_HEADS, dtype=np.float64)
    log_gamma = np.log1p(-np.exp2(-5.0 - hd))
    idx = np.arange(RET_CHUNK, dtype=np.float64)
    diff = idx[:, None] - idx[None, :]
    dmask = np.where(diff >= 0, np.exp(log_gamma[:, None, None] * np.maximum(diff, 0.0)), 0.0)
    ones = np.ones((1, 1, RET_HEAD_DIM))
    qdec = np.exp(log_gamma[:, None] * (idx + 1.0))[:, :, None] * ones
    kdec = np.exp(log_gamma[:, None] * (RET_CHUNK - 1 - idx))[:, :, None] * ones
    chunk_decay = tuple(float(x) for x in np.exp(log_gamma * RET_CHUNK))
    as_f32 = lambda a: jnp.asarray(a, dtype=F32)
    return as_f32(dmask), as_f32(qdec), as_f32(kdec), chunk_decay


def _layer(h, p, positions, ln_ffn1, w_ffn1_gate, w_ffn1_up, w_ffn1_down, ln_mix, w_in, b_forget, w_merge,
           b_merge, w_ret_out, w_fox_out, w_out, ln_ffn2, w_ffn2_gate, w_ffn2_up, w_ffn2_down, ln_ple, w_ple,
           w_ple_gate, ln_final):
    b, s, d = h.shape
    t = b * s
    tm = min(TOKEN_TILE, s)
    row = lambda v: v.reshape(1, -1).astype(F32)

    inv_freq = 1.0 / (ROPE_BASE ** (jnp.arange(RET_HEAD_DIM // 2, dtype=F32) / (RET_HEAD_DIM // 2)))
    invf = jnp.concatenate([inv_freq, inv_freq]).reshape(1, RET_HEAD_DIM)
    sgn = jnp.concatenate([-jnp.ones((RET_HEAD_DIM // 2,), F32), jnp.ones((RET_HEAD_DIM // 2,), F32)]).reshape(1, -1)
    h1 = _ffn(h.reshape(t, d), row(ln_ffn1), w_ffn1_gate.astype(BF16), w_ffn1_up.astype(BF16),
              w_ffn1_down.astype(BF16), tm)

    even_odd = lambda w: w.reshape(d, RET_HEADS, RET_HEAD_DIM // 2, 2).transpose(0, 1, 3, 2).reshape(d, RET_WIDTH)
    w_rq = even_odd(w_in[:, :RET_WIDTH].astype(BF16))
    w_rk = even_odd(w_in[:, RET_WIDTH:2 * RET_WIDTH].astype(BF16))
    fox_lo = 4 * RET_WIDTH
    fox_hi = fox_lo + 3 * FOX_WIDTH
    w_ff = jnp.pad(w_in[:, fox_hi:], ((0, 0), (0, LANES - FOX_HEADS)))
    wr = jnp.concatenate([w_ff, w_rq, w_rk, w_in[:, 2 * RET_WIDTH:fox_lo]], axis=1).astype(BF16)
    w_fq, w_fk, w_fv = (w_in[:, fox_lo + i * FOX_WIDTH:fox_lo + (i + 1) * FOX_WIDTH] for i in range(3))
    wqvt = jnp.concatenate([w_fq, w_fv], axis=1).T.astype(BF16)
    bfg = jnp.pad(b_forget.astype(F32), (0, LANES - FOX_HEADS)).reshape(1, LANES)
    r, fk, fqvt, gates = _mix_in(h1.reshape(b, s, d), row(ln_mix), positions.reshape(b, s, 1), invf, sgn, wr,
                                 w_fk.astype(BF16), wqvt, bfg, w_merge.astype(BF16), row(b_merge), tm)

    dmask, qdec, kdec, chunk_decay = _retention_tables()
    y_ret = _retention(r, dmask, qdec, kdec, chunk_decay, min(RET_TILE, s))
    y_fox = _fox(fk, fqvt, min(FOX_TILE, s))

    assert ln_final is not None, "multi-layer stacks need a post kernel without the final norm"
    out = _post(h1, y_ret.reshape(t, RET_WIDTH), y_fox, gates.reshape(t, 2 * d),
                p.reshape(t, -1), w_ret_out.astype(BF16), w_fox_out.astype(BF16), w_out.astype(BF16), row(ln_ffn2),
                w_ffn2_gate.astype(BF16), w_ffn2_up.astype(BF16), w_ffn2_down.astype(BF16), row(ln_ple),
                w_ple_gate.astype(BF16), w_ple.astype(BF16), row(ln_final), tm)
    return out.reshape(b, s, d)


def kernel(x, p, positions, ln_ffn1, w_ffn1_gate, w_ffn1_up, w_ffn1_down, ln_mix, w_in, b_forget, w_merge, b_merge,
           w_ret_out, w_fox_out, w_out, ln_ffn2, w_ffn2_gate, w_ffn2_up, w_ffn2_down, ln_ple, w_ple, w_ple_gate,
           ln_final):
    depth = p.shape[0]
    assert depth == 1, "this implementation covers the single-layer configuration"
    return _layer(x, p[0], positions, ln_ffn1[0], w_ffn1_gate[0], w_ffn1_up[0], w_ffn1_down[0], ln_mix[0], w_in[0],
                  b_forget[0], w_merge[0], b_merge[0], w_ret_out[0], w_fox_out[0], w_out[0], ln_ffn2[0],
                  w_ffn2_gate[0], w_ffn2_up[0], w_ffn2_down[0], ln_ple[0], w_ple[0], w_ple_gate[0], ln_final)
```
